```python
import math
import jax, jax.numpy as jnp
from jax import lax
import numpy as np

D_MODEL = 4096
BATCH = 2
SEQ = 4096
DEPTH = 1

N_META = 16
ATTN_WIDTH = D_MODEL // 2
LRU_WIDTH = D_MODEL - ATTN_WIDTH
DIFF_QK_DIM = 128
DIFF_V_DIM = 2 * DIFF_QK_DIM
DIFF_HEADS = ATTN_WIDTH // DIFF_V_DIM
LRU_HEADS = 16
LRU_BLOCK = LRU_WIDTH // LRU_HEADS
CONV_WIDTH = 4
LRU_C = 8.0
D_FF = ((-(-8 * D_MODEL // 3)) + 255) // 256 * 256
IN_WIDTH = 3 * ATTN_WIDTH + 2 * LRU_WIDTH
ROPE_THETA = 10000.0
BLOCK_Q = 128
RMS_EPS = 1e-6

kernel_name = 'hymba_diffattn_rglru_sandwich'


def rms_norm(x, g):
    xf = x.astype(jnp.float32)
    y = xf * lax.rsqrt(jnp.mean(xf * xf, axis=-1, keepdims=True) + RMS_EPS)
    return (y * g.astype(jnp.float32)).astype(x.dtype)


def rotary_tables(T):
    inv_freq = 1.0 / (ROPE_THETA ** (jnp.arange(0, DIFF_QK_DIM, 2, dtype=jnp.float32) / DIFF_QK_DIM))
    ang = jnp.arange(T, dtype=jnp.float32)[:, None] * inv_freq[None, :]
    return jnp.cos(ang), jnp.sin(ang)


def apply_rope(x, cos, sin):
    half = x.shape[-1] // 2
    xf = x.astype(jnp.float32)
    x1, x2 = xf[..., :half], xf[..., half:]
    c = cos[None, :, None, None, :]
    s = sin[None, :, None, None, :]
    return jnp.concatenate([x1 * c - x2 * s, x2 * c + x1 * s], axis=-1).astype(x.dtype)


def diff_attention(q, k, v, lam, subln_g, lambda_init):
    B, T = q.shape[0], q.shape[1]
    scale = DIFF_QK_DIM ** -0.5
    outs = []
    for start in range(0, T, BLOCK_Q):
        end = start + BLOCK_Q
        qb = q[:, start:end]
        kb = k[:, :end]
        vb = v[:, :end]
        s = jnp.einsum('bqhcd,bkhcd->bhcqk', qb, kb).astype(jnp.float32) * scale
        qpos = jnp.arange(start, end)
        kpos = jnp.arange(end)
        mask = kpos[None, :] <= qpos[:, None]
        s = jnp.where(mask, s, -jnp.inf)
        p = jax.nn.softmax(s, axis=-1)
        w = p[:, :, 0] - lam * p[:, :, 1]
        outs.append(jnp.einsum('bhqk,bkhd->bqhd', w.astype(v.dtype), vb))
    o = jnp.concatenate(outs, axis=1)
    o = rms_norm(o, subln_g) * (1.0 - lambda_init)
    return o.reshape(B, T, DIFF_HEADS * DIFF_V_DIM)


def rg_lru_branch(xr, gate, conv_w, conv_b, w_r, b_r, w_i, b_i, lru_lambda):
    B, T, C = xr.shape
    xc = lax.conv_general_dilated(
        xr, conv_w[:, None, :].astype(xr.dtype), window_strides=(1,),
        padding=[(CONV_WIDTH - 1, 0)], dimension_numbers=('NWC', 'WIO', 'NWC'),
        feature_group_count=C) + conv_b
    xh = xc.reshape(B, T, LRU_HEADS, LRU_BLOCK)
    r = jax.nn.sigmoid(jnp.einsum('bthi,hij->bthj', xh, w_r).reshape(B, T, C) + b_r)
    i = jax.nn.sigmoid(jnp.einsum('bthi,hij->bthj', xh, w_i).reshape(B, T, C) + b_i)
    log_a = -LRU_C * r.astype(jnp.float32) * jax.nn.softplus(-lru_lambda.astype(jnp.float32))
    a = jnp.exp(log_a)
    mult = jnp.sqrt(-jnp.expm1(2.0 * log_a))
    bterm = mult * (i * xc).astype(jnp.float32)

    def combine(left, right):
        a_l, b_l = left
        a_r, b_r2 = right
        return a_l * a_r, a_r * b_l + b_r2

    _, h = lax.associative_scan(combine, (a, bterm), axis=1)
    return h.astype(xr.dtype) * jax.nn.gelu(gate)


def setup_inputs(seed: int = 0) -> dict:
    key = jax.random.key(seed)
    ks = jax.random.split(key, 24)
    f32 = jnp.float32
    nrm = lambda k, shp, sc: jax.random.normal(k, shp, f32) * sc
    u = jax.random.uniform(ks[14], (DEPTH, LRU_WIDTH), f32, 0.9, 0.999)
    a0 = u ** (1.0 / LRU_C)
    lru_lambda = jnp.log(a0) - jnp.log1p(-a0)
    return {
        'x': nrm(ks[0], (BATCH, SEQ, D_MODEL), 1.0),
        'meta_tokens': nrm(ks[1], (N_META, D_MODEL), 1.0),
        'mix_pre_g': 1.0 + nrm(ks[2], (DEPTH, D_MODEL), 0.02),
        'w_in': nrm(ks[3], (DEPTH, D_MODEL, IN_WIDTH), D_MODEL ** -0.5),
        'lambda_q1': nrm(ks[4], (DEPTH, DIFF_QK_DIM), 0.1),
        'lambda_k1': nrm(ks[5], (DEPTH, DIFF_QK_DIM), 0.1),
        'lambda_q2': nrm(ks[6], (DEPTH, DIFF_QK_DIM), 0.1),
        'lambda_k2': nrm(ks[7], (DEPTH, DIFF_QK_DIM), 0.1),
        'subln_g': 1.0 + nrm(ks[8], (DEPTH, DIFF_V_DIM), 0.02),
        'conv_w': nrm(ks[9], (DEPTH, CONV_WIDTH, LRU_WIDTH), CONV_WIDTH ** -0.5),
        'conv_b': nrm(ks[10], (DEPTH, LRU_WIDTH), 0.02),
        'w_r': nrm(ks[11], (DEPTH, LRU_HEADS, LRU_BLOCK, LRU_BLOCK), LRU_BLOCK ** -0.5),
        'b_r': nrm(ks[12], (DEPTH, LRU_WIDTH), 0.02),
        'w_i': nrm(ks[13], (DEPTH, LRU_HEADS, LRU_BLOCK, LRU_BLOCK), LRU_BLOCK ** -0.5),
        'b_i': nrm(ks[15], (DEPTH, LRU_WIDTH), 0.02),
        'lru_lambda': lru_lambda,
        'w_out': nrm(ks[16], (DEPTH, ATTN_WIDTH + LRU_WIDTH, D_MODEL), (ATTN_WIDTH + LRU_WIDTH) ** -0.5),
        'mix_post_g': 1.0 + nrm(ks[17], (DEPTH, D_MODEL), 0.02),
        'ffn_pre_g': 1.0 + nrm(ks[18], (DEPTH, D_MODEL), 0.02),
        'w_gate': nrm(ks[19], (DEPTH, D_MODEL, D_FF), D_MODEL ** -0.5),
        'w_up': nrm(ks[20], (DEPTH, D_MODEL, D_FF), D_MODEL ** -0.5),
        'w_down': nrm(ks[21], (DEPTH, D_FF, D_MODEL), D_FF ** -0.5),
        'ffn_post_g': 1.0 + nrm(ks[22], (DEPTH, D_MODEL), 0.02),
    }


def reference(x, meta_tokens, mix_pre_g, w_in, lambda_q1, lambda_k1, lambda_q2, lambda_k2,
              subln_g, conv_w, conv_b, w_r, b_r, w_i, b_i, lru_lambda, w_out, mix_post_g,
              ffn_pre_g, w_gate, w_up, w_down, ffn_post_g):
    B, S, D = x.shape
    T = N_META + S
    T_pad = -(-T // BLOCK_Q) * BLOCK_Q
    meta = jnp.broadcast_to(meta_tokens[None].astype(x.dtype), (B, N_META, D))
    h = jnp.concatenate([meta, x, jnp.zeros((B, T_pad - T, D), x.dtype)], axis=1)
    cos, sin = rotary_tables(T_pad)
    splits = [ATTN_WIDTH, 2 * ATTN_WIDTH, 3 * ATTN_WIDTH, 3 * ATTN_WIDTH + LRU_WIDTH]
    for l in range(DEPTH):
        lambda_init = 0.8 - 0.6 * math.exp(-0.3 * l)
        u = rms_norm(h, mix_pre_g[l])
        proj = u @ w_in[l]
        q, k, v, xr, gate = jnp.split(proj, splits, axis=-1)
        q = apply_rope(q.reshape(B, T_pad, DIFF_HEADS, 2, DIFF_QK_DIM), cos, sin)
        k = apply_rope(k.reshape(B, T_pad, DIFF_HEADS, 2, DIFF_QK_DIM), cos, sin)
        v = v.reshape(B, T_pad, DIFF_HEADS, DIFF_V_DIM)
        lam = (jnp.exp(jnp.sum(lambda_q1[l].astype(jnp.float32) * lambda_k1[l].astype(jnp.float32)))
               - jnp.exp(jnp.sum(lambda_q2[l].astype(jnp.float32) * lambda_k2[l].astype(jnp.float32)))
               + lambda_init)
        attn = diff_attention(q, k, v, lam, subln_g[l], lambda_init)
        rec = rg_lru_branch(xr, gate, conv_w[l], conv_b[l], w_r[l], b_r[l],
                            w_i[l], b_i[l], lru_lambda[l])
        mixed = jnp.concatenate([attn, rec], axis=-1) @ w_out[l]
        h = h + rms_norm(mixed, mix_post_g[l])
        u = rms_norm(h, ffn_pre_g[l])
        f = (jax.nn.silu(u @ w_gate[l]) * (u @ w_up[l])) @ w_down[l]
        h = h + rms_norm(f, ffn_post_g[l])
    return h[:, N_META:N_META + S]
```

```python
import functools
import math

import jax
import jax.numpy as jnp
from jax import lax
from jax.experimental import pallas as pl
from jax.experimental.pallas import tpu as pltpu

DIFF_QK_DIM = 128
DIFF_V_DIM = 2 * DIFF_QK_DIM
LRU_HEADS = 16
CONV_WIDTH = 4
LRU_C = 8.0
ROPE_THETA = 10000.0
RMS_EPS = 1e-6

LANES = 128
SUBLANES = 8
VMEM_LIMIT_BYTES = 56 * 1024 * 1024

BF16 = jnp.bfloat16
F32 = jnp.float32


def _params(n_axes):
    return pltpu.CompilerParams(
        dimension_semantics=("arbitrary",) * n_axes,
        vmem_limit_bytes=VMEM_LIMIT_BYTES,
    )


def _rms_scale(x):
    return lax.rsqrt(jnp.mean(x * x, axis=-1, keepdims=True) + RMS_EPS)


def _cast_into(dst_ref, src_ref, chunk=256):
    rows = src_ref.shape[0]
    chunk = min(chunk, rows)
    assert rows % chunk == 0

    def body(c, carry):
        r0 = pl.multiple_of(c * chunk, chunk)
        dst_ref[pl.ds(r0, chunk), :] = src_ref[pl.ds(r0, chunk), :].astype(dst_ref.dtype)
        return carry

    lax.fori_loop(0, rows // chunk, body, 0)


def _prenorm_kernel(x_ref, g_ref, o_ref):
    x = x_ref[...]
    o_ref[...] = (x * _rms_scale(x) * g_ref[...]).astype(o_ref.dtype)


def _prenorm(x2d, g, tm):
    rows, d = x2d.shape
    return pl.pallas_call(
        _prenorm_kernel,
        grid=(rows // tm,),
        in_specs=[pl.BlockSpec((tm, d), lambda i: (i, 0)),
                  pl.BlockSpec((1, d), lambda i: (0, 0))],
        out_specs=pl.BlockSpec((tm, d), lambda i: (i, 0)),
        out_shape=jax.ShapeDtypeStruct((rows, d), BF16),
        compiler_params=_params(1),
        name="prenorm",
    )(x2d, g.reshape(1, d))


def _mm_kernel(a_ref, w_ref, *rest, epilogue, n_extra):
    extra = rest[:n_extra]
    o_ref = rest[n_extra]
    wbf = rest[n_extra + 1]

    @pl.when(pl.program_id(1) == 0)
    def _():
        _cast_into(wbf, w_ref)

    acc = jnp.dot(a_ref[...], wbf[...], preferred_element_type=F32)
    epilogue(acc, o_ref, *extra)


def _plain_epilogue(acc, o_ref):
    o_ref[...] = acc.astype(o_ref.dtype)


def _rope_epilogue(acc, o_ref, cos_ref, sin_ref, *, n_scaled, scale):
    j = pl.program_id(0)
    mult = jnp.where(j < n_scaled, jnp.float32(scale), jnp.float32(1.0))
    cos = cos_ref[...] * mult
    sin = sin_ref[...] * mult
    for c in range(acc.shape[1] // DIFF_QK_DIM):
        sl = slice(c * DIFF_QK_DIM, (c + 1) * DIFF_QK_DIM)
        y = acc[:, sl]
        o_ref[:, sl] = (y * cos + pltpu.roll(y, DIFF_QK_DIM // 2, 1) * sin).astype(o_ref.dtype)


def _matmul_ws(a, w, col0, ncols, *, tm, tn, out_dtype, epilogue=_plain_epilogue,
               extras=(), extra_specs=(), name):
    m, k = a.shape
    assert w.shape[0] == k and m % tm == 0 and ncols % tn == 0 and col0 % tn == 0
    jb0 = col0 // tn
    kern = functools.partial(_mm_kernel, epilogue=epilogue, n_extra=len(extras))
    return pl.pallas_call(
        kern,
        grid=(ncols // tn, m // tm),
        in_specs=[pl.BlockSpec((tm, k), lambda j, i: (i, 0)),
                  pl.BlockSpec((k, tn), lambda j, i: (0, jb0 + j)),
                  *extra_specs],
        out_specs=pl.BlockSpec((tm, tn), lambda j, i: (i, j)),
        out_shape=jax.ShapeDtypeStruct((m, ncols), out_dtype),
        scratch_shapes=[pltpu.VMEM((k, tn), BF16)],
        compiler_params=_params(2),
        name=name,
    )(a, w, *extras)


def _dot_nt(a, b):
    return lax.dot_general(a, b, (((1,), (1,)), ((), ())), preferred_element_type=F32)


def _attn_kernel(lq1_ref, lk1_ref, lq2_ref, lk2_ref, g_ref, q_ref, k_ref, v_ref, km_ref, vm_ref,
                 o_ref, acc_sc, m_sc, l_sc, *, tq, tk, lambda_init):
    qi = pl.program_id(2)
    dk = DIFF_QK_DIM

    def update(c, s, v):
        m_prev = m_sc[c]
        m_new = jnp.maximum(m_prev, jnp.max(s, axis=1, keepdims=True))
        alpha = jnp.exp(m_prev - m_new)
        p = jnp.exp(s - m_new)
        l_sc[c] = alpha * l_sc[c] + jnp.sum(p, axis=1, keepdims=True)
        acc_sc[c] = alpha * acc_sc[c] + jnp.dot(p.astype(BF16), v, preferred_element_type=F32)
        m_sc[c] = m_new

    for c in range(2):
        cs = slice(c * dk, (c + 1) * dk)
        qc = q_ref[:, cs]
        s = _dot_nt(qc, km_ref[:, cs])
        m0 = jnp.max(s, axis=1, keepdims=True)
        p = jnp.exp(s - m0)
        m_sc[c] = m0
        l_sc[c] = jnp.sum(p, axis=1, keepdims=True)
        acc_sc[c] = jnp.dot(p.astype(BF16), vm_ref[...], preferred_element_type=F32)

        def body(kt, carry, c=c, cs=cs, qc=qc):
            ks = pl.multiple_of(kt * tk, tk)
            s = _dot_nt(qc, k_ref[pl.ds(ks, tk), cs])
            update(c, s, v_ref[pl.ds(ks, tk), :])
            return carry

        lax.fori_loop(0, qi * (tq // tk), body, 0)

        for dt in range(tq // tk):
            ks = pl.multiple_of(qi * tq + dt * tk, tk)
            s = _dot_nt(qc, k_ref[pl.ds(ks, tk), cs])
            row = lax.broadcasted_iota(jnp.int32, (tq, tk), 0)
            col = lax.broadcasted_iota(jnp.int32, (tq, tk), 1) + dt * tk
            s = jnp.where(col <= row, s, -jnp.inf)
            update(c, s, v_ref[pl.ds(ks, tk), :])

    lam = (jnp.exp(jnp.sum(lq1_ref[...] * lk1_ref[...], axis=1, keepdims=True))
           - jnp.exp(jnp.sum(lq2_ref[...] * lk2_ref[...], axis=1, keepdims=True))
           + lambda_init)
    o = acc_sc[0] / l_sc[0] - lam * (acc_sc[1] / l_sc[1])
    o = o * _rms_scale(o) * g_ref[...] * (1.0 - lambda_init)
    o_ref[...] = o.astype(o_ref.dtype)


def _diff_attention(qk, v, k_meta, v_meta, lq1, lk1, lq2, lk2, subln_g, *, batch, seq, heads,
                    lambda_init, tq, tk):
    dv = DIFF_V_DIM
    nq = seq // tq
    n_meta = k_meta.shape[0]
    vec = lambda a: a.reshape(1, -1).astype(F32)
    small = lambda n: pl.BlockSpec((1, n), lambda b, h, i: (0, 0))
    kern = functools.partial(_attn_kernel, tq=tq, tk=tk, lambda_init=lambda_init)
    return pl.pallas_call(
        kern,
        grid=(batch, heads, nq),
        in_specs=[small(DIFF_QK_DIM)] * 4 + [
            small(dv),
            pl.BlockSpec((tq, dv), lambda b, h, i: (b * nq + i, h)),
            pl.BlockSpec((seq, dv), lambda b, h, i: (b, heads + h)),
            pl.BlockSpec((seq, dv), lambda b, h, i: (b, h)),
            pl.BlockSpec((n_meta, dv), lambda b, h, i: (0, h)),
            pl.BlockSpec((n_meta, dv), lambda b, h, i: (0, h)),
        ],
        out_specs=pl.BlockSpec((tq, dv), lambda b, h, i: (b * nq + i, h)),
        out_shape=jax.ShapeDtypeStruct((batch * seq, heads * dv), BF16),
        scratch_shapes=[pltpu.VMEM((2, tq, dv), F32),
                        pltpu.VMEM((2, tq, 1), F32),
                        pltpu.VMEM((2, tq, 1), F32)],
        compiler_params=_params(3),
        name="diff_attention",
    )(vec(lq1), vec(lk1), vec(lq2), vec(lk2), vec(subln_g), qk, qk, v, k_meta, v_meta)


def _gelu_tanh(x):
    c = math.sqrt(2.0 / math.pi)
    return 0.5 * x * (1.0 + jnp.tanh(c * (x + 0.044715 * (x * x * x))))


def _lru_kernel(xr_ref, gate_ref, tail0_ref, h0_ref, cw_ref, cb_ref, wr_ref, br_ref, wi_ref,
                bi_ref, lam_ref, o_ref, hlast_ref, xs_sc, h_sc, *, tt, hc):
    t = pl.program_id(2)
    blk = LANES
    halo = SUBLANES

    @pl.when(t == 0)
    def _():
        xs_sc[0:halo, :] = tail0_ref[...]
        h_sc[...] = h0_ref[...]

    xs_sc[halo:halo + tt, :] = xr_ref[...]
    xc = cb_ref[...]
    for w in range(CONV_WIDTH):
        start = halo - (CONV_WIDTH - 1) + w
        xc = xc + cw_ref[w:w + 1, :] * xs_sc[pl.ds(start, tt), :]
    xs_sc[0:halo, :] = xr_ref[tt - halo:tt, :]

    r_parts, i_parts = [], []
    for hh in range(hc):
        xh = xc[:, hh * blk:(hh + 1) * blk].astype(BF16)
        r_parts.append(jnp.dot(xh, wr_ref[hh].astype(BF16), preferred_element_type=F32))
        i_parts.append(jnp.dot(xh, wi_ref[hh].astype(BF16), preferred_element_type=F32))
    r = jax.nn.sigmoid(jnp.concatenate(r_parts, axis=1) + br_ref[...])
    ig = jax.nn.sigmoid(jnp.concatenate(i_parts, axis=1) + bi_ref[...])

    neg_lam = -lam_ref[...]
    softplus = jnp.maximum(neg_lam, 0.0) + jnp.log1p(jnp.exp(-jnp.abs(neg_lam)))
    log_a = (-LRU_C) * r * softplus
    a = jnp.exp(log_a)
    mult = jnp.sqrt(-jnp.tanh(log_a) * (a * a + 1.0))
    b = mult * (ig * xc)

    row = lax.broadcasted_iota(jnp.int32, a.shape, 0)
    d = 1
    while d < tt:
        keep = row >= d
        a_sh = jnp.where(keep, pltpu.roll(a, d, 0), 1.0)
        b_sh = jnp.where(keep, pltpu.roll(b, d, 0), 0.0)
        b = a * b_sh + b
        a = a * a_sh
        d *= 2
    h = a * h_sc[0:1, :] + b
    h_sc[...] = jnp.broadcast_to(h[tt - 1:tt, :], h_sc.shape)
    hlast_ref[...] = h_sc[...]
    o_ref[...] = (h * _gelu_tanh(gate_ref[...])).astype(o_ref.dtype)


def _rg_lru(xg, tail0, h0, conv_w, conv_b, w_r, b_r, w_i, b_i, lru_lambda, *, batch, seq, tt, hc):
    width = conv_w.shape[1]
    cw = hc * LANES
    ngrp = width // cw
    nt = seq // tt
    row = lambda a: a.reshape(1, width).astype(F32)
    chan = lambda r: pl.BlockSpec((r, cw), lambda b, g, t: (0, g))
    wspec = pl.BlockSpec((hc, LANES, LANES), lambda b, g, t: (g, 0, 0))
    kern = functools.partial(_lru_kernel, tt=tt, hc=hc)
    return pl.pallas_call(
        kern,
        grid=(batch, ngrp, nt),
        in_specs=[
            pl.BlockSpec((tt, cw), lambda b, g, t: (b * nt + t, g)),
            pl.BlockSpec((tt, cw), lambda b, g, t: (b * nt + t, ngrp + g)),
            chan(SUBLANES), chan(SUBLANES),
            chan(CONV_WIDTH), chan(1),
            wspec, chan(1), wspec, chan(1), chan(1),
        ],
        out_specs=[pl.BlockSpec((tt, cw), lambda b, g, t: (b * nt + t, g)),
                   pl.BlockSpec((SUBLANES, cw), lambda b, g, t: (b, g))],
        out_shape=[jax.ShapeDtypeStruct((batch * seq, width), BF16),
                   jax.ShapeDtypeStruct((batch * SUBLANES, width), F32)],
        scratch_shapes=[pltpu.VMEM((tt + SUBLANES, cw), F32),
                        pltpu.VMEM((SUBLANES, cw), F32)],
        compiler_params=_params(3),
        name="rg_lru",
    )(xg, xg, tail0, h0, conv_w.astype(F32), row(conv_b), w_r, row(b_r), w_i, row(b_i),
      row(lru_lambda))


def _mm2_kernel(a1_ref, a2_ref, w1_ref, w2_ref, o_ref, wbf1, wbf2):
    @pl.when(pl.program_id(1) == 0)
    def _():
        _cast_into(wbf1, w1_ref)
        _cast_into(wbf2, w2_ref)

    acc = jnp.dot(a1_ref[...], wbf1[...], preferred_element_type=F32)
    acc = acc + jnp.dot(a2_ref[...], wbf2[...], preferred_element_type=F32)
    o_ref[...] = acc


def _out_proj(attn, rec, w_out, *, tm, tn):
    m, k1 = attn.shape
    k2 = rec.shape[1]
    assert k1 == k2
    n = w_out.shape[1]
    return pl.pallas_call(
        _mm2_kernel,
        grid=(n // tn, m // tm),
        in_specs=[pl.BlockSpec((tm, k1), lambda j, i: (i, 0)),
                  pl.BlockSpec((tm, k2), lambda j, i: (i, 0)),
                  pl.BlockSpec((k1, tn), lambda j, i: (0, j)),
                  pl.BlockSpec((k2, tn), lambda j, i: (1, j))],
        out_specs=pl.BlockSpec((tm, tn), lambda j, i: (i, j)),
        out_shape=jax.ShapeDtypeStruct((m, n), F32),
        scratch_shapes=[pltpu.VMEM((k1, tn), BF16), pltpu.VMEM((k2, tn), BF16)],
        compiler_params=_params(2),
        name="out_proj",
    )(attn, rec, w_out, w_out)


def _mix_residual_kernel(mixed_ref, x_ref, g1_ref, g2_ref, h1_ref, u2_ref):
    mixed = mixed_ref[...]
    h1 = x_ref[...] + mixed * _rms_scale(mixed) * g1_ref[...]
    h1_ref[...] = h1
    u2_ref[...] = (h1 * _rms_scale(h1) * g2_ref[...]).astype(u2_ref.dtype)


def _mix_residual(mixed, x2d, g_post, g_pre, tm):
    rows, d = x2d.shape
    big = pl.BlockSpec((tm, d), lambda i: (i, 0))
    gspec = pl.BlockSpec((1, d), lambda i: (0, 0))
    return pl.pallas_call(
        _mix_residual_kernel,
        grid=(rows // tm,),
        in_specs=[big, big, gspec, gspec],
        out_specs=[big, big],
        out_shape=[jax.ShapeDtypeStruct((rows, d), F32), jax.ShapeDtypeStruct((rows, d), BF16)],
        compiler_params=_params(1),
        name="mix_residual",
    )(mixed, x2d, g_post.reshape(1, d), g_pre.reshape(1, d))


def _ffn_up_kernel(a_ref, wg_ref, wu_ref, o_ref, wgb, wub):
    @pl.when(pl.program_id(1) == 0)
    def _():
        _cast_into(wgb, wg_ref)
        _cast_into(wub, wu_ref)

    a = a_ref[...]
    g = jnp.dot(a, wgb[...], preferred_element_type=F32)
    u = jnp.dot(a, wub[...], preferred_element_type=F32)
    o_ref[...] = (g * jax.nn.sigmoid(g) * u).astype(o_ref.dtype)


def _ffn_up(u2, w_gate, w_up, *, tm, tf):
    m, k = u2.shape
    f = w_gate.shape[1]
    wspec = pl.BlockSpec((k, tf), lambda j, i: (0, j))
    return pl.pallas_call(
        _ffn_up_kernel,
        grid=(f // tf, m // tm),
        in_specs=[pl.BlockSpec((tm, k), lambda j, i: (i, 0)), wspec, wspec],
        out_specs=pl.BlockSpec((tm, tf), lambda j, i: (i, j)),
        out_shape=jax.ShapeDtypeStruct((m, f), BF16),
        scratch_shapes=[pltpu.VMEM((k, tf), BF16), pltpu.VMEM((k, tf), BF16)],
        compiler_params=_params(2),
        name="ffn_up",
    )(u2, w_gate, w_up)


def _ffn_down_kernel(a_ref, w_ref, o_ref, wbf):
    @pl.when(pl.program_id(2) == 0)
    def _():
        _cast_into(wbf, w_ref, chunk=128)

    o_ref[0] = jnp.dot(a_ref[...], wbf[...], preferred_element_type=F32)


def _ffn_down(act, w_down, *, tm, tn, ksplit):
    m, f = act.shape
    n = w_down.shape[1]
    tk = f // ksplit
    assert f % ksplit == 0 and tk % LANES == 0
    return pl.pallas_call(
        _ffn_down_kernel,
        grid=(ksplit, n // tn, m // tm),
        in_specs=[pl.BlockSpec((tm, tk), lambda s, j, i: (i, s)),
                  pl.BlockSpec((tk, tn), lambda s, j, i: (s, j))],
        out_specs=pl.BlockSpec((1, tm, tn), lambda s, j, i: (s, i, j)),
        out_shape=jax.ShapeDtypeStruct((ksplit, m, n), F32),
        scratch_shapes=[pltpu.VMEM((tk, tn), BF16)],
        compiler_params=_params(3),
        name="ffn_down",
    )(act, w_down)


def _ffn_residual_kernel(f_ref, h1_ref, g_ref, o_ref):
    f = f_ref[0]
    for s in range(1, f_ref.shape[0]):
        f = f + f_ref[s]
    o_ref[...] = h1_ref[...] + f * _rms_scale(f) * g_ref[...]


def _ffn_residual(fparts, h1, g, tm):
    ksplit, rows, d = fparts.shape
    big = pl.BlockSpec((tm, d), lambda i: (i, 0))
    return pl.pallas_call(
        _ffn_residual_kernel,
        grid=(rows // tm,),
        in_specs=[pl.BlockSpec((ksplit, tm, d), lambda i: (0, i, 0)), big,
                  pl.BlockSpec((1, d), lambda i: (0, 0))],
        out_specs=big,
        out_shape=jax.ShapeDtypeStruct((rows, d), F32),
        compiler_params=_params(1),
        name="ffn_residual",
    )(fparts, h1, g.reshape(1, d))


def _rope_tables(n_pos):
    inv_freq = 1.0 / (ROPE_THETA ** (jnp.arange(0, DIFF_QK_DIM, 2, dtype=F32) / DIFF_QK_DIM))
    ang = jnp.arange(n_pos, dtype=F32)[:, None] * inv_freq[None, :]
    cos, sin = jnp.cos(ang), jnp.sin(ang)
    return jnp.concatenate([cos, cos], axis=-1), jnp.concatenate([-sin, sin], axis=-1)


def kernel(x, meta_tokens, mix_pre_g, w_in, lambda_q1, lambda_k1, lambda_q2, lambda_k2, subln_g,
           conv_w, conv_b, w_r, b_r, w_i, b_i, lru_lambda, w_out, mix_post_g, ffn_pre_g, w_gate,
           w_up, w_down, ffn_post_g):
    batch, seq, d = x.shape
    n_meta = meta_tokens.shape[0]
    depth = w_in.shape[0]
    attn_w = d // 2
    lru_w = d - attn_w
    heads = attn_w // DIFF_V_DIM
    rows = batch * seq
    assert depth == 1, "the meta-prefix factorisation below is written for a single layer"
    assert n_meta == 2 * SUBLANES and seq % 1024 == 0

    l = 0
    lambda_init = 0.8 - 0.6 * math.exp(-0.3 * l)
    scale = DIFF_QK_DIM ** -0.5
    cos_all, sin_all = _rope_tables(n_meta + seq)
    cos_x, sin_x = cos_all[n_meta:], sin_all[n_meta:]
    cos_m, sin_m = cos_all[:n_meta], sin_all[:n_meta]

    x2d = x.reshape(rows, d)
    w_in_l = w_in[l]

    tm, tn = 1024, 512
    seq_tiles = seq // tm
    rope_x = functools.partial(_rope_epilogue, n_scaled=attn_w // tn, scale=scale)
    rope_m = functools.partial(_rope_epilogue, n_scaled=0, scale=scale)
    tab_x = pl.BlockSpec((tm, DIFF_QK_DIM), lambda j, i: (i % seq_tiles, 0))
    tab_m = pl.BlockSpec((n_meta, DIFF_QK_DIM), lambda j, i: (0, 0))

    u_x = _prenorm(x2d, mix_pre_g[l], 256)
    u_m = _prenorm(meta_tokens.astype(x.dtype), mix_pre_g[l], n_meta)

    qk_x = _matmul_ws(u_x, w_in_l, 0, 2 * attn_w, tm=tm, tn=tn, out_dtype=BF16, epilogue=rope_x,
                      extras=(cos_x, sin_x), extra_specs=(tab_x, tab_x), name="in_proj_qk")
    v_x = _matmul_ws(u_x, w_in_l, 2 * attn_w, attn_w, tm=tm, tn=tn, out_dtype=BF16,
                     name="in_proj_v")
    xg_x = _matmul_ws(u_x, w_in_l, 3 * attn_w, 2 * lru_w, tm=tm, tn=tn, out_dtype=F32,
                      name="in_proj_lru")

    k_m = _matmul_ws(u_m, w_in_l, attn_w, attn_w, tm=n_meta, tn=tn, out_dtype=BF16,
                     epilogue=rope_m, extras=(cos_m, sin_m), extra_specs=(tab_m, tab_m),
                     name="in_proj_k_meta")
    v_m = _matmul_ws(u_m, w_in_l, 2 * attn_w, attn_w, tm=n_meta, tn=tn, out_dtype=BF16,
                     name="in_proj_v_meta")
    xg_m = _matmul_ws(u_m, w_in_l, 3 * attn_w, 2 * lru_w, tm=n_meta, tn=tn, out_dtype=F32,
                      name="in_proj_lru_meta")

    attn = _diff_attention(qk_x, v_x, k_m, v_m, lambda_q1[l], lambda_k1[l], lambda_q2[l],
                           lambda_k2[l], subln_g[l], batch=batch, seq=seq, heads=heads,
                           lambda_init=lambda_init, tq=512, tk=512)

    lru_args = (conv_w[l], conv_b[l], w_r[l], b_r[l], w_i[l], b_i[l], lru_lambda[l])
    zeros8 = jnp.zeros((SUBLANES, lru_w), F32)
    _, h_meta = _rg_lru(xg_m, zeros8, zeros8, *lru_args, batch=1, seq=n_meta, tt=n_meta, hc=4)
    tail_meta = xg_m[n_meta - SUBLANES:, :lru_w]
    rec, _ = _rg_lru(xg_x, tail_meta, h_meta, *lru_args, batch=batch, seq=seq, tt=256, hc=4)

    mixed = _out_proj(attn, rec, w_out[l], tm=tm, tn=tn)
    h1, u2 = _mix_residual(mixed, x2d, mix_post_g[l], ffn_pre_g[l], 256)

    act = _ffn_up(u2, w_gate[l], w_up[l], tm=tm, tf=256)
    fparts = _ffn_down(act, w_down[l], tm=512, tn=512, ksplit=2)
    out = _ffn_residual(fparts, h1, ffn_post_g[l], 256)
    return out.reshape(batch, seq, d)
```

```python
import functools
import math

import jax
import jax.numpy as jnp
from jax import lax
from jax.experimental import pallas as pl
from jax.experimental.pallas import tpu as pltpu

DIFF_QK_DIM = 128
DIFF_V_DIM = 2 * DIFF_QK_DIM
LRU_HEADS = 16
CONV_WIDTH = 4
LRU_C = 8.0
ROPE_THETA = 10000.0
RMS_EPS = 1e-6

LANES = 128
SUBLANES = 8
VMEM_LIMIT_BYTES = 56 * 1024 * 1024

BF16 = jnp.bfloat16
F32 = jnp.float32


def _params(n_axes):
    return pltpu.CompilerParams(
        dimension_semantics=("arbitrary",) * n_axes,
        vmem_limit_bytes=VMEM_LIMIT_BYTES,
    )


def _rms_scale(x):
    return lax.rsqrt(jnp.mean(x * x, axis=-1, keepdims=True) + RMS_EPS)


def _cast_into(dst_ref, src_ref, chunk=256):
    rows = src_ref.shape[0]
    chunk = min(chunk, rows)
    assert rows % chunk == 0

    def body(c, carry):
        r0 = pl.multiple_of(c * chunk, chunk)
        dst_ref[pl.ds(r0, chunk), :] = src_ref[pl.ds(r0, chunk), :].astype(dst_ref.dtype)
        return carry

    lax.fori_loop(0, rows // chunk, body, 0)


def _prenorm_kernel(x_ref, g_ref, o_ref):
    x = x_ref[...]
    o_ref[...] = (x * _rms_scale(x) * g_ref[...]).astype(o_ref.dtype)


def _prenorm(x2d, g, tm):
    rows, d = x2d.shape
    return pl.pallas_call(
        _prenorm_kernel,
        grid=(rows // tm,),
        in_specs=[pl.BlockSpec((tm, d), lambda i: (i, 0)),
                  pl.BlockSpec((1, d), lambda i: (0, 0))],
        out_specs=pl.BlockSpec((tm, d), lambda i: (i, 0)),
        out_shape=jax.ShapeDtypeStruct((rows, d), BF16),
        compiler_params=_params(1),
        name="prenorm",
    )(x2d, g.reshape(1, d))


def _mm_kernel(a_ref, w_ref, *rest, epilogue, n_extra):
    extra = rest[:n_extra]
    o_ref = rest[n_extra]
    wbf = rest[n_extra + 1]

    @pl.when(pl.program_id(1) == 0)
    def _():
        _cast_into(wbf, w_ref)

    acc = jnp.dot(a_ref[...], wbf[...], preferred_element_type=F32)
    epilogue(acc, o_ref, *extra)


def _plain_epilogue(acc, o_ref):
    o_ref[...] = acc.astype(o_ref.dtype)


def _rope_epilogue(acc, o_ref, cos_ref, sin_ref, *, n_scaled, scale):
    j = pl.program_id(0)
    mult = jnp.where(j < n_scaled, jnp.float32(scale), jnp.float32(1.0))
    cos = cos_ref[...] * mult
    sin = sin_ref[...] * mult
    for c in range(acc.shape[1] // DIFF_QK_DIM):
        sl = slice(c * DIFF_QK_DIM, (c + 1) * DIFF_QK_DIM)
        y = acc[:, sl]
        o_ref[:, sl] = (y * cos + pltpu.roll(y, DIFF_QK_DIM // 2, 1) * sin).astype(o_ref.dtype)


def _matmul_ws(a, w, col0, ncols, *, tm, tn, out_dtype, epilogue=_plain_epilogue,
               extras=(), extra_specs=(), name):
    m, k = a.shape
    assert w.shape[0] == k and m % tm == 0 and ncols % tn == 0 and col0 % tn == 0
    jb0 = col0 // tn
    kern = functools.partial(_mm_kernel, epilogue=epilogue, n_extra=len(extras))
    return pl.pallas_call(
        kern,
        grid=(ncols // tn, m // tm),
        in_specs=[pl.BlockSpec((tm, k), lambda j, i: (i, 0)),
                  pl.BlockSpec((k, tn), lambda j, i: (0, jb0 + j)),
                  *extra_specs],
        out_specs=pl.BlockSpec((tm, tn), lambda j, i: (i, j)),
        out_shape=jax.ShapeDtypeStruct((m, ncols), out_dtype),
        scratch_shapes=[pltpu.VMEM((k, tn), BF16)],
        compiler_params=_params(2),
        name=name,
    )(a, w, *extras)


def _dot_nt(a, b):
    return lax.dot_general(a, b, (((1,), (1,)), ((), ())), preferred_element_type=F32)


def _attn_kernel(lq1_ref, lk1_ref, lq2_ref, lk2_ref, g_ref, q_ref, k_ref, v_ref, km_ref, vm_ref,
                 o_ref, acc_sc, m_sc, l_sc, *, tq, tk, n_meta, lambda_init):
    assert tq == tk
    qi = pl.program_id(2)
    dk = DIFF_QK_DIM
    n_meta_pad = km_ref.shape[0]

    def update(c, blocks, first):
        groups = [s[:, g * LANES:(g + 1) * LANES] for s, _ in blocks
                  for g in range(s.shape[1] // LANES)]
        mx = groups[0]
        for grp in groups[1:]:
            mx = jnp.maximum(mx, grp)
        m_cur = jnp.max(mx, axis=1, keepdims=True)
        if first:
            m_new = jnp.broadcast_to(m_cur, (tq, LANES))
        else:
            m_prev = m_sc[c]
            m_new = jnp.maximum(m_prev, m_cur)
            alpha = jnp.exp2(m_prev - m_new)
        ps = [jnp.exp2(grp - m_new) for grp in groups]
        lsum = ps[0]
        for p in ps[1:]:
            lsum = lsum + p
        pv = None
        g0 = 0
        for s, v in blocks:
            ng = s.shape[1] // LANES
            p = jnp.concatenate([p.astype(BF16) for p in ps[g0:g0 + ng]], axis=1)
            g0 += ng
            d = jnp.dot(p, v, preferred_element_type=F32)
            pv = d if pv is None else pv + d
        if first:
            l_sc[c] = lsum
            acc_sc[c] = pv
        else:
            l_sc[c] = alpha * l_sc[c] + lsum
            acc_sc[c] = jnp.concatenate([alpha] * (DIFF_V_DIM // LANES), axis=1) * acc_sc[c] + pv
        m_sc[c] = m_new

    ks_diag = pl.multiple_of(qi * tq, tq)
    meta_ok = lax.broadcasted_iota(jnp.int32, (tq, n_meta_pad), 1) < n_meta
    causal = (lax.broadcasted_iota(jnp.int32, (tq, tk), 1)
              <= lax.broadcasted_iota(jnp.int32, (tq, tk), 0))
    for c in range(2):
        cs = slice(c * dk, (c + 1) * dk)
        q = q_ref[:, cs]
        s_meta = jnp.where(meta_ok, _dot_nt(q, km_ref[:, cs]), -jnp.inf)
        s_diag = jnp.where(causal, _dot_nt(q, k_ref[pl.ds(ks_diag, tk), cs]), -jnp.inf)
        update(c, [(s_meta, vm_ref[...]), (s_diag, v_ref[pl.ds(ks_diag, tk), :])], True)

    def tile(ks):
        v = v_ref[pl.ds(ks, tk), :]
        for c in range(2):
            cs = slice(c * dk, (c + 1) * dk)
            update(c, [(_dot_nt(q_ref[:, cs], k_ref[pl.ds(ks, tk), cs]), v)], False)

    def body(pair, carry):
        tile(pl.multiple_of(2 * pair * tk, tk))
        tile(pl.multiple_of((2 * pair + 1) * tk, tk))
        return carry

    lax.fori_loop(0, qi // 2, body, 0)

    @pl.when(qi % 2 == 1)
    def _():
        tile(pl.multiple_of((qi - 1) * tk, tk))

    lam = (jnp.exp(jnp.sum(lq1_ref[...] * lk1_ref[...], axis=1, keepdims=True))
           - jnp.exp(jnp.sum(lq2_ref[...] * lk2_ref[...], axis=1, keepdims=True))
           + lambda_init)
    l0 = jnp.sum(l_sc[0], axis=1, keepdims=True)
    l1 = jnp.sum(l_sc[1], axis=1, keepdims=True)
    o = acc_sc[0] / l0 - lam * (acc_sc[1] / l1)
    o = o * _rms_scale(o) * g_ref[...] * (1.0 - lambda_init)
    o_ref[...] = o.astype(o_ref.dtype)


def _diff_attention(qk, v, k_meta, v_meta, lq1, lk1, lq2, lk2, subln_g, *, batch, seq, heads,
                    lambda_init, tq, tk):
    dv = DIFF_V_DIM
    nq = seq // tq
    n_meta = k_meta.shape[0]
    n_meta_pad = LANES
    assert n_meta <= n_meta_pad
    k_meta = jnp.pad(k_meta, ((0, n_meta_pad - n_meta), (0, 0)))
    v_meta = jnp.pad(v_meta, ((0, n_meta_pad - n_meta), (0, 0)))
    vec = lambda a: a.reshape(1, -1).astype(F32)
    small = lambda n: pl.BlockSpec((1, n), lambda b, h, i: (0, 0))
    kern = functools.partial(_attn_kernel, tq=tq, tk=tk, n_meta=n_meta, lambda_init=lambda_init)
    return pl.pallas_call(
        kern,
        grid=(batch, heads, nq),
        in_specs=[small(DIFF_QK_DIM)] * 4 + [
            small(dv),
            pl.BlockSpec((tq, dv), lambda b, h, i: (b * nq + i, h)),
            pl.BlockSpec((seq, dv), lambda b, h, i: (b, heads + h)),
            pl.BlockSpec((seq, dv), lambda b, h, i: (b, h)),
            pl.BlockSpec((n_meta_pad, dv), lambda b, h, i: (0, h)),
            pl.BlockSpec((n_meta_pad, dv), lambda b, h, i: (0, h)),
        ],
        out_specs=pl.BlockSpec((tq, dv), lambda b, h, i: (b * nq + i, h)),
        out_shape=jax.ShapeDtypeStruct((batch * seq, heads * dv), BF16),
        scratch_shapes=[pltpu.VMEM((2, tq, dv), F32),
                        pltpu.VMEM((2, tq, LANES), F32),
                        pltpu.VMEM((2, tq, LANES), F32)],
        compiler_params=_params(3),
        name="diff_attention",
    )(vec(lq1), vec(lk1), vec(lq2), vec(lk2), vec(subln_g), qk, qk, v, k_meta, v_meta)


def _gelu_tanh(x):
    c = math.sqrt(2.0 / math.pi)
    return 0.5 * x * (1.0 + jnp.tanh(c * (x + 0.044715 * (x * x * x))))


def _sigmoid(x):
    return 0.5 * jnp.tanh(0.5 * x) + 0.5


def _lru_kernel(xr_ref, gate_ref, tail0_ref, h0_ref, cw_ref, cb_ref, wr_ref, br_ref, wi_ref,
                bi_ref, lam_ref, o_ref, hlast_ref, xs_sc, h_sc, *, tt, hc):
    t = pl.program_id(2)
    blk = LANES
    halo = SUBLANES

    @pl.when(t == 0)
    def _():
        xs_sc[0:halo, :] = tail0_ref[...]
        h_sc[...] = h0_ref[...]

    xs_sc[halo:halo + tt, :] = xr_ref[...]
    xc = cb_ref[...]
    for w in range(CONV_WIDTH):
        start = halo - (CONV_WIDTH - 1) + w
        xc = xc + cw_ref[w:w + 1, :] * xs_sc[pl.ds(start, tt), :]
    xs_sc[0:halo, :] = xr_ref[tt - halo:tt, :]

    r_parts, i_parts = [], []
    for hh in range(hc):
        xh = xc[:, hh * blk:(hh + 1) * blk].astype(BF16)
        r_parts.append(jnp.dot(xh, wr_ref[hh].astype(BF16), preferred_element_type=F32))
        i_parts.append(jnp.dot(xh, wi_ref[hh].astype(BF16), preferred_element_type=F32))
    r = _sigmoid(jnp.concatenate(r_parts, axis=1) + br_ref[...])
    ig = _sigmoid(jnp.concatenate(i_parts, axis=1) + bi_ref[...])

    neg_lam = -lam_ref[...]
    softplus = jnp.maximum(neg_lam, 0.0) + jnp.log1p(jnp.exp(-jnp.abs(neg_lam)))
    log_a = (-LRU_C) * r * softplus
    a = jnp.exp(log_a)
    mult = jnp.sqrt(-jnp.tanh(log_a) * (a * a + 1.0))
    b = mult * (ig * xc)

    ngroups = tt // SUBLANES
    a = a.reshape(ngroups, SUBLANES, a.shape[1])
    b = b.reshape(a.shape)
    row = lax.broadcasted_iota(jnp.int32, (1,) + a.shape[1:], 1)
    d = 1
    while d < SUBLANES:
        keep = row >= d
        a_sh = jnp.where(keep, pltpu.roll(a, d, 1), 1.0)
        b_sh = jnp.where(keep, pltpu.roll(b, d, 1), 0.0)
        b = a * b_sh + b
        a = a * a_sh
        d *= 2
    carry = h_sc[...]
    groups = []
    for j in range(ngroups):
        hj = a[j] * carry + b[j]
        groups.append(hj)
        carry = jnp.broadcast_to(hj[SUBLANES - 1:SUBLANES], hj.shape)
    h = jnp.concatenate(groups, axis=0)
    h_sc[...] = carry
    hlast_ref[...] = carry
    o_ref[...] = (h * _gelu_tanh(gate_ref[...])).astype(o_ref.dtype)


def _rg_lru(xg, tail0, h0, conv_w, conv_b, w_r, b_r, w_i, b_i, lru_lambda, *, batch, seq, tt, hc):
    width = conv_w.shape[1]
    cw = hc * LANES
    ngrp = width // cw
    nt = seq // tt
    row = lambda a: a.reshape(1, width).astype(F32)
    chan = lambda r: pl.BlockSpec((r, cw), lambda b, g, t: (0, g))
    wspec = pl.BlockSpec((hc, LANES, LANES), lambda b, g, t: (g, 0, 0))
    kern = functools.partial(_lru_kernel, tt=tt, hc=hc)
    return pl.pallas_call(
        kern,
        grid=(batch, ngrp, nt),
        in_specs=[
            pl.BlockSpec((tt, cw), lambda b, g, t: (b * nt + t, g)),
            pl.BlockSpec((tt, cw), lambda b, g, t: (b * nt + t, ngrp + g)),
            chan(SUBLANES), chan(SUBLANES),
            chan(CONV_WIDTH), chan(1),
            wspec, chan(1), wspec, chan(1), chan(1),
        ],
        out_specs=[pl.BlockSpec((tt, cw), lambda b, g, t: (b * nt + t, g)),
                   pl.BlockSpec((SUBLANES, cw), lambda b, g, t: (b, g))],
        out_shape=[jax.ShapeDtypeStruct((batch * seq, width), BF16),
                   jax.ShapeDtypeStruct((batch * SUBLANES, width), F32)],
        scratch_shapes=[pltpu.VMEM((tt + SUBLANES, cw), F32),
                        pltpu.VMEM((SUBLANES, cw), F32)],
        compiler_params=_params(3),
        name="rg_lru",
    )(xg, xg, tail0, h0, conv_w.astype(F32), row(conv_b), w_r, row(b_r), w_i, row(b_i),
      row(lru_lambda))


def _mm2_kernel(a1_ref, a2_ref, w1_ref, w2_ref, o_ref, wbf1, wbf2):
    @pl.when(pl.program_id(1) == 0)
    def _():
        _cast_into(wbf1, w1_ref)
        _cast_into(wbf2, w2_ref)

    acc = jnp.dot(a1_ref[...], wbf1[...], preferred_element_type=F32)
    acc = acc + jnp.dot(a2_ref[...], wbf2[...], preferred_element_type=F32)
    o_ref[...] = acc


def _out_proj(attn, rec, w_out, *, tm, tn):
    m, k1 = attn.shape
    k2 = rec.shape[1]
    assert k1 == k2
    n = w_out.shape[1]
    return pl.pallas_call(
        _mm2_kernel,
        grid=(n // tn, m // tm),
        in_specs=[pl.BlockSpec((tm, k1), lambda j, i: (i, 0)),
                  pl.BlockSpec((tm, k2), lambda j, i: (i, 0)),
                  pl.BlockSpec((k1, tn), lambda j, i: (0, j)),
                  pl.BlockSpec((k2, tn), lambda j, i: (1, j))],
        out_specs=pl.BlockSpec((tm, tn), lambda j, i: (i, j)),
        out_shape=jax.ShapeDtypeStruct((m, n), F32),
        scratch_shapes=[pltpu.VMEM((k1, tn), BF16), pltpu.VMEM((k2, tn), BF16)],
        compiler_params=_params(2),
        name="out_proj",
    )(attn, rec, w_out, w_out)


def _mix_residual_kernel(mixed_ref, x_ref, g1_ref, g2_ref, h1_ref, u2_ref):
    mixed = mixed_ref[...]
    h1 = x_ref[...] + mixed * _rms_scale(mixed) * g1_ref[...]
    h1_ref[...] = h1
    u2_ref[...] = (h1 * _rms_scale(h1) * g2_ref[...]).astype(u2_ref.dtype)


def _mix_residual(mixed, x2d, g_post, g_pre, tm):
    rows, d = x2d.shape
    big = pl.BlockSpec((tm, d), lambda i: (i, 0))
    gspec = pl.BlockSpec((1, d), lambda i: (0, 0))
    return pl.pallas_call(
        _mix_residual_kernel,
        grid=(rows // tm,),
        in_specs=[big, big, gspec, gspec],
        out_specs=[big, big],
        out_shape=[jax.ShapeDtypeStruct((rows, d), F32), jax.ShapeDtypeStruct((rows, d), BF16)],
        compiler_params=_params(1),
        name="mix_residual",
    )(mixed, x2d, g_post.reshape(1, d), g_pre.reshape(1, d))


def _ffn_up_kernel(a_ref, wg_ref, wu_ref, o_ref, wgb, wub):
    @pl.when(pl.program_id(1) == 0)
    def _():
        _cast_into(wgb, wg_ref)
        _cast_into(wub, wu_ref)

    a = a_ref[...]
    g = jnp.dot(a, wgb[...], preferred_element_type=F32)
    u = jnp.dot(a, wub[...], preferred_element_type=F32)
    o_ref[...] = (g * jax.nn.sigmoid(g) * u).astype(o_ref.dtype)


def _ffn_up(u2, w_gate, w_up, *, tm, tf):
    m, k = u2.shape
    f = w_gate.shape[1]
    wspec = pl.BlockSpec((k, tf), lambda j, i: (0, j))
    return pl.pallas_call(
        _ffn_up_kernel,
        grid=(f // tf, m // tm),
        in_specs=[pl.BlockSpec((tm, k), lambda j, i: (i, 0)), wspec, wspec],
        out_specs=pl.BlockSpec((tm, tf), lambda j, i: (i, j)),
        out_shape=jax.ShapeDtypeStruct((m, f), BF16),
        scratch_shapes=[pltpu.VMEM((k, tf), BF16), pltpu.VMEM((k, tf), BF16)],
        compiler_params=_params(2),
        name="ffn_up",
    )(u2, w_gate, w_up)


def _ffn_down_kernel(a_ref, w_ref, o_ref, wbf):
    @pl.when(pl.program_id(2) == 0)
    def _():
        _cast_into(wbf, w_ref, chunk=128)

    o_ref[0] = jnp.dot(a_ref[...], wbf[...], preferred_element_type=F32)


def _ffn_down(act, w_down, *, tm, tn, ksplit):
    m, f = act.shape
    n = w_down.shape[1]
    tk = f // ksplit
    assert f % ksplit == 0 and tk % LANES == 0
    return pl.pallas_call(
        _ffn_down_kernel,
        grid=(ksplit, n // tn, m // tm),
        in_specs=[pl.BlockSpec((tm, tk), lambda s, j, i: (i, s)),
                  pl.BlockSpec((tk, tn), lambda s, j, i: (s, j))],
        out_specs=pl.BlockSpec((1, tm, tn), lambda s, j, i: (s, i, j)),
        out_shape=jax.ShapeDtypeStruct((ksplit, m, n), F32),
        scratch_shapes=[pltpu.VMEM((tk, tn), BF16)],
        compiler_params=_params(3),
        name="ffn_down",
    )(act, w_down)


def _ffn_residual_kernel(f_ref, h1_ref, g_ref, o_ref):
    f = f_ref[0]
    for s in range(1, f_ref.shape[0]):
        f = f + f_ref[s]
    o_ref[...] = h1_ref[...] + f * _rms_scale(f) * g_ref[...]


def _ffn_residual(fparts, h1, g, tm):
    ksplit, rows, d = fparts.shape
    big = pl.BlockSpec((tm, d), lambda i: (i, 0))
    return pl.pallas_call(
        _ffn_residual_kernel,
        grid=(rows // tm,),
        in_specs=[pl.BlockSpec((ksplit, tm, d), lambda i: (0, i, 0)), big,
                  pl.BlockSpec((1, d), lambda i: (0, 0))],
        out_specs=big,
        out_shape=jax.ShapeDtypeStruct((rows, d), F32),
        compiler_params=_params(1),
        name="ffn_residual",
    )(fparts, h1, g.reshape(1, d))


def _rope_tables(n_pos):
    inv_freq = 1.0 / (ROPE_THETA ** (jnp.arange(0, DIFF_QK_DIM, 2, dtype=F32) / DIFF_QK_DIM))
    ang = jnp.arange(n_pos, dtype=F32)[:, None] * inv_freq[None, :]
    cos, sin = jnp.cos(ang), jnp.sin(ang)
    return jnp.concatenate([cos, cos], axis=-1), jnp.concatenate([-sin, sin], axis=-1)


def kernel(x, meta_tokens, mix_pre_g, w_in, lambda_q1, lambda_k1, lambda_q2, lambda_k2, subln_g,
           conv_w, conv_b, w_r, b_r, w_i, b_i, lru_lambda, w_out, mix_post_g, ffn_pre_g, w_gate,
           w_up, w_down, ffn_post_g):
    batch, seq, d = x.shape
    n_meta = meta_tokens.shape[0]
    depth = w_in.shape[0]
    attn_w = d // 2
    lru_w = d - attn_w
    heads = attn_w // DIFF_V_DIM
    rows = batch * seq
    assert depth == 1, "the meta-prefix factorisation below is written for a single layer"
    assert n_meta == 2 * SUBLANES and seq % 1024 == 0

    l = 0
    lambda_init = 0.8 - 0.6 * math.exp(-0.3 * l)
    scale = DIFF_QK_DIM ** -0.5 * math.log2(math.e)
    cos_all, sin_all = _rope_tables(n_meta + seq)
    cos_x, sin_x = cos_all[n_meta:], sin_all[n_meta:]
    cos_m, sin_m = cos_all[:n_meta], sin_all[:n_meta]

    x2d = x.reshape(rows, d)
    w_in_l = w_in[l]

    tm, tn = 1024, 512
    seq_tiles = seq // tm
    rope_x = functools.partial(_rope_epilogue, n_scaled=attn_w // tn, scale=scale)
    rope_m = functools.partial(_rope_epilogue, n_scaled=0, scale=scale)
    tab_x = pl.BlockSpec((tm, DIFF_QK_DIM), lambda j, i: (i % seq_tiles, 0))
    tab_m = pl.BlockSpec((n_meta, DIFF_QK_DIM), lambda j, i: (0, 0))

    u_x = _prenorm(x2d, mix_pre_g[l], 256)
    u_m = _prenorm(meta_tokens.astype(x.dtype), mix_pre_g[l], n_meta)

    qk_x = _matmul_ws(u_x, w_in_l, 0, 2 * attn_w, tm=tm, tn=tn, out_dtype=BF16, epilogue=rope_x,
                      extras=(cos_x, sin_x), extra_specs=(tab_x, tab_x), name="in_proj_qk")
    v_x = _matmul_ws(u_x, w_in_l, 2 * attn_w, attn_w, tm=tm, tn=tn, out_dtype=BF16,
                     name="in_proj_v")
    xg_x = _matmul_ws(u_x, w_in_l, 3 * attn_w, 2 * lru_w, tm=tm, tn=tn, out_dtype=F32,
                      name="in_proj_lru")

    k_m = _matmul_ws(u_m, w_in_l, attn_w, attn_w, tm=n_meta, tn=tn, out_dtype=BF16,
                     epilogue=rope_m, extras=(cos_m, sin_m), extra_specs=(tab_m, tab_m),
                     name="in_proj_k_meta")
    v_m = _matmul_ws(u_m, w_in_l, 2 * attn_w, attn_w, tm=n_meta, tn=tn, out_dtype=BF16,
                     name="in_proj_v_meta")
    xg_m = _matmul_ws(u_m, w_in_l, 3 * attn_w, 2 * lru_w, tm=n_meta, tn=tn, out_dtype=F32,
                      name="in_proj_lru_meta")

    attn = _diff_attention(qk_x, v_x, k_m, v_m, lambda_q1[l], lambda_k1[l], lambda_q2[l],
                           lambda_k2[l], subln_g[l], batch=batch, seq=seq, heads=heads,
                           lambda_init=lambda_init, tq=512, tk=512)

    lru_args = (conv_w[l], conv_b[l], w_r[l], b_r[l], w_i[l], b_i[l], lru_lambda[l])
    zeros8 = jnp.zeros((SUBLANES, lru_w), F32)
    _, h_meta = _rg_lru(xg_m, zeros8, zeros8, *lru_args, batch=1, seq=n_meta, tt=n_meta, hc=4)
    tail_meta = xg_m[n_meta - SUBLANES:, :lru_w]
    rec, _ = _rg_lru(xg_x, tail_meta, h_meta, *lru_args, batch=batch, seq=seq, tt=512, hc=4)

    mixed = _out_proj(attn, rec, w_out[l], tm=tm, tn=tn)
    h1, u2 = _mix_residual(mixed, x2d, mix_post_g[l], ffn_pre_g[l], 256)

    act = _ffn_up(u2, w_gate[l], w_up[l], tm=tm, tf=256)
    fparts = _ffn_down(act, w_down[l], tm=512, tn=512, ksplit=2)
    out = _ffn_residual(fparts, h1, ffn_post_g[l], 256)
    return out.reshape(batch, seq, d)
```

```python
import functools
import math

import jax
import jax.numpy as jnp
from jax import lax
from jax.experimental import pallas as pl
from jax.experimental.pallas import tpu as pltpu

DIFF_QK_DIM = 128
DIFF_V_DIM = 2 * DIFF_QK_DIM
LRU_HEADS = 16
CONV_WIDTH = 4
LRU_C = 8.0
ROPE_THETA = 10000.0
RMS_EPS = 1e-6

LANES = 128
SUBLANES = 8
VMEM_LIMIT_BYTES = 56 * 1024 * 1024

BF16 = jnp.bfloat16
F32 = jnp.float32


def _params(n_axes):
    return pltpu.CompilerParams(
        dimension_semantics=("arbitrary",) * n_axes,
        vmem_limit_bytes=VMEM_LIMIT_BYTES,
    )


def _rms_scale(x):
    return lax.rsqrt(jnp.mean(x * x, axis=-1, keepdims=True) + RMS_EPS)


def _cast_into(dst_ref, src_ref, chunk=256, rows=None):
    rows = src_ref.shape[0] if rows is None else rows
    chunk = min(chunk, rows)
    assert rows % chunk == 0

    def body(c, carry):
        r0 = pl.multiple_of(c * chunk, chunk)
        dst_ref[pl.ds(r0, chunk), :] = src_ref[pl.ds(r0, chunk), :].astype(dst_ref.dtype)
        return carry

    lax.fori_loop(0, rows // chunk, body, 0)


def _prenorm_kernel(x_ref, g_ref, o_ref):
    x = x_ref[...]
    o_ref[...] = (x * _rms_scale(x) * g_ref[...]).astype(o_ref.dtype)


def _prenorm(x2d, g, tm):
    rows, d = x2d.shape
    return pl.pallas_call(
        _prenorm_kernel,
        grid=(rows // tm,),
        in_specs=[pl.BlockSpec((tm, d), lambda i: (i, 0)),
                  pl.BlockSpec((1, d), lambda i: (0, 0))],
        out_specs=pl.BlockSpec((tm, d), lambda i: (i, 0)),
        out_shape=jax.ShapeDtypeStruct((rows, d), BF16),
        compiler_params=_params(1),
        name="prenorm",
    )(x2d, g.reshape(1, d))


def _mm_kernel(a_ref, w_ref, *rest, epilogue, n_extra):
    extra = rest[:n_extra]
    o_ref = rest[n_extra]
    wbf = rest[n_extra + 1]

    @pl.when(pl.program_id(1) == 0)
    def _():
        _cast_into(wbf, w_ref)

    acc = jnp.dot(a_ref[...], wbf[...], preferred_element_type=F32)
    epilogue(acc, o_ref, *extra)


def _plain_epilogue(acc, o_ref):
    o_ref[...] = acc.astype(o_ref.dtype)


def _rope_epilogue(acc, o_ref, cos_ref, sin_ref, *, n_scaled, scale):
    j = pl.program_id(0)
    mult = jnp.where(j < n_scaled, jnp.float32(scale), jnp.float32(1.0))
    cos = cos_ref[...] * mult
    sin = sin_ref[...] * mult
    for c in range(acc.shape[1] // DIFF_QK_DIM):
        sl = slice(c * DIFF_QK_DIM, (c + 1) * DIFF_QK_DIM)
        y = acc[:, sl]
        o_ref[:, sl] = (y * cos + pltpu.roll(y, DIFF_QK_DIM // 2, 1) * sin).astype(o_ref.dtype)


def _matmul_ws(a, w, col0, ncols, *, tm, tn, out_dtype, epilogue=_plain_epilogue,
               extras=(), extra_specs=(), name):
    m, k = a.shape
    assert w.shape[0] == k and m % tm == 0 and ncols % tn == 0 and col0 % tn == 0
    jb0 = col0 // tn
    kern = functools.partial(_mm_kernel, epilogue=epilogue, n_extra=len(extras))
    return pl.pallas_call(
        kern,
        grid=(ncols // tn, m // tm),
        in_specs=[pl.BlockSpec((tm, k), lambda j, i: (i, 0)),
                  pl.BlockSpec((k, tn), lambda j, i: (0, jb0 + j)),
                  *extra_specs],
        out_specs=pl.BlockSpec((tm, tn), lambda j, i: (i, j)),
        out_shape=jax.ShapeDtypeStruct((m, ncols), out_dtype),
        scratch_shapes=[pltpu.VMEM((k, tn), BF16)],
        compiler_params=_params(2),
        name=name,
    )(a, w, *extras)


def _dot_nt(a, b):
    return lax.dot_general(a, b, (((1,), (1,)), ((), ())), preferred_element_type=F32)


def _attn_kernel(lq1_ref, lk1_ref, lq2_ref, lk2_ref, g_ref, q_ref, k_ref, v_ref, km_ref, vm_ref,
                 o_ref, acc_sc, m_sc, l_sc, *, tq, tk, n_meta, lambda_init):
    assert tq == tk
    qi = pl.program_id(2)
    dk = DIFF_QK_DIM
    n_meta_pad = km_ref.shape[0]

    def update(c, blocks, first):
        groups = [s[:, g * LANES:(g + 1) * LANES] for s, _ in blocks
                  for g in range(s.shape[1] // LANES)]
        mx = groups[0]
        for grp in groups[1:]:
            mx = jnp.maximum(mx, grp)
        m_cur = jnp.max(mx, axis=1, keepdims=True)
        if first:
            m_new = jnp.broadcast_to(m_cur, (tq, LANES))
        else:
            m_prev = m_sc[c]
            m_new = jnp.maximum(m_prev, m_cur)
            alpha = jnp.exp2(m_prev - m_new)
        ps = [jnp.exp2(grp - m_new) for grp in groups]
        lsum = ps[0]
        for p in ps[1:]:
            lsum = lsum + p
        pv = None
        g0 = 0
        for s, v in blocks:
            ng = s.shape[1] // LANES
            p = jnp.concatenate([p.astype(BF16) for p in ps[g0:g0 + ng]], axis=1)
            g0 += ng
            d = jnp.dot(p, v, preferred_element_type=F32)
            pv = d if pv is None else pv + d
        if first:
            l_sc[c] = lsum
            acc_sc[c] = pv
        else:
            l_sc[c] = alpha * l_sc[c] + lsum
            acc_sc[c] = jnp.concatenate([alpha] * (DIFF_V_DIM // LANES), axis=1) * acc_sc[c] + pv
        m_sc[c] = m_new

    ks_diag = pl.multiple_of(qi * tq, tq)
    meta_ok = lax.broadcasted_iota(jnp.int32, (tq, n_meta_pad), 1) < n_meta
    causal = (lax.broadcasted_iota(jnp.int32, (tq, tk), 1)
              <= lax.broadcasted_iota(jnp.int32, (tq, tk), 0))
    for c in range(2):
        cs = slice(c * dk, (c + 1) * dk)
        q = q_ref[:, cs]
        s_meta = jnp.where(meta_ok, _dot_nt(q, km_ref[:, cs]), -jnp.inf)
        s_diag = jnp.where(causal, _dot_nt(q, k_ref[pl.ds(ks_diag, tk), cs]), -jnp.inf)
        update(c, [(s_meta, vm_ref[...]), (s_diag, v_ref[pl.ds(ks_diag, tk), :])], True)

    def tile(ks):
        v = v_ref[pl.ds(ks, tk), :]
        for c in range(2):
            cs = slice(c * dk, (c + 1) * dk)
            update(c, [(_dot_nt(q_ref[:, cs], k_ref[pl.ds(ks, tk), cs]), v)], False)

    def body(pair, carry):
        tile(pl.multiple_of(2 * pair * tk, tk))
        tile(pl.multiple_of((2 * pair + 1) * tk, tk))
        return carry

    lax.fori_loop(0, qi // 2, body, 0)

    @pl.when(qi % 2 == 1)
    def _():
        tile(pl.multiple_of((qi - 1) * tk, tk))

    lam = (jnp.exp(jnp.sum(lq1_ref[...] * lk1_ref[...], axis=1, keepdims=True))
           - jnp.exp(jnp.sum(lq2_ref[...] * lk2_ref[...], axis=1, keepdims=True))
           + lambda_init)
    l0 = jnp.sum(l_sc[0], axis=1, keepdims=True)
    l1 = jnp.sum(l_sc[1], axis=1, keepdims=True)
    o = acc_sc[0] / l0 - lam * (acc_sc[1] / l1)
    o = o * _rms_scale(o) * g_ref[...] * (1.0 - lambda_init)
    o_ref[...] = o.astype(o_ref.dtype)


def _diff_attention(qk, v, k_meta, v_meta, lq1, lk1, lq2, lk2, subln_g, *, batch, seq, heads,
                    lambda_init, tq, tk):
    dv = DIFF_V_DIM
    nq = seq // tq
    n_meta = k_meta.shape[0]
    n_meta_pad = LANES
    assert n_meta <= n_meta_pad
    k_meta = jnp.pad(k_meta, ((0, n_meta_pad - n_meta), (0, 0)))
    v_meta = jnp.pad(v_meta, ((0, n_meta_pad - n_meta), (0, 0)))
    vec = lambda a: a.reshape(1, -1).astype(F32)
    small = lambda n: pl.BlockSpec((1, n), lambda b, h, i: (0, 0))
    kern = functools.partial(_attn_kernel, tq=tq, tk=tk, n_meta=n_meta, lambda_init=lambda_init)
    return pl.pallas_call(
        kern,
        grid=(batch, heads, nq),
        in_specs=[small(DIFF_QK_DIM)] * 4 + [
            small(dv),
            pl.BlockSpec((tq, dv), lambda b, h, i: (b * nq + i, h)),
            pl.BlockSpec((seq, dv), lambda b, h, i: (b, heads + h)),
            pl.BlockSpec((seq, dv), lambda b, h, i: (b, h)),
            pl.BlockSpec((n_meta_pad, dv), lambda b, h, i: (0, h)),
            pl.BlockSpec((n_meta_pad, dv), lambda b, h, i: (0, h)),
        ],
        out_specs=pl.BlockSpec((tq, dv), lambda b, h, i: (b * nq + i, h)),
        out_shape=jax.ShapeDtypeStruct((batch * seq, heads * dv), BF16),
        scratch_shapes=[pltpu.VMEM((2, tq, dv), F32),
                        pltpu.VMEM((2, tq, LANES), F32),
                        pltpu.VMEM((2, tq, LANES), F32)],
        compiler_params=_params(3),
        name="diff_attention",
    )(vec(lq1), vec(lk1), vec(lq2), vec(lk2), vec(subln_g), qk, qk, v, k_meta, v_meta)


def _gelu_tanh(x):
    c = math.sqrt(2.0 / math.pi)
    return 0.5 * x * (1.0 + jnp.tanh(c * (x + 0.044715 * (x * x * x))))


def _sigmoid(x):
    return 0.5 * jnp.tanh(0.5 * x) + 0.5


def _lru_kernel(xr_ref, gate_ref, tail0_ref, h0_ref, cw_ref, cb_ref, wr_ref, br_ref, wi_ref,
                bi_ref, lam_ref, o_ref, hlast_ref, xs_sc, h_sc, *, tt, hc):
    t = pl.program_id(2)
    blk = LANES
    halo = SUBLANES

    @pl.when(t == 0)
    def _():
        xs_sc[0:halo, :] = tail0_ref[...]
        h_sc[...] = h0_ref[...]

    xs_sc[halo:halo + tt, :] = xr_ref[...]
    xc = cb_ref[...]
    for w in range(CONV_WIDTH):
        start = halo - (CONV_WIDTH - 1) + w
        xc = xc + cw_ref[w:w + 1, :] * xs_sc[pl.ds(start, tt), :]
    xs_sc[0:halo, :] = xr_ref[tt - halo:tt, :]

    r_parts, i_parts = [], []
    for hh in range(hc):
        xh = xc[:, hh * blk:(hh + 1) * blk].astype(BF16)
        r_parts.append(jnp.dot(xh, wr_ref[hh].astype(BF16), preferred_element_type=F32))
        i_parts.append(jnp.dot(xh, wi_ref[hh].astype(BF16), preferred_element_type=F32))
    r = _sigmoid(jnp.concatenate(r_parts, axis=1) + br_ref[...])
    ig = _sigmoid(jnp.concatenate(i_parts, axis=1) + bi_ref[...])

    neg_lam = -lam_ref[...]
    softplus = jnp.maximum(neg_lam, 0.0) + jnp.log1p(jnp.exp(-jnp.abs(neg_lam)))
    log_a = (-LRU_C) * r * softplus
    a = jnp.exp(log_a)
    mult = jnp.sqrt(-jnp.tanh(log_a) * (a * a + 1.0))
    b = mult * (ig * xc)

    ngroups = tt // SUBLANES
    a = a.reshape(ngroups, SUBLANES, a.shape[1])
    b = b.reshape(a.shape)
    row = lax.broadcasted_iota(jnp.int32, (1,) + a.shape[1:], 1)
    d = 1
    while d < SUBLANES:
        keep = row >= d
        a_sh = jnp.where(keep, pltpu.roll(a, d, 1), 1.0)
        b_sh = jnp.where(keep, pltpu.roll(b, d, 1), 0.0)
        b = a * b_sh + b
        a = a * a_sh
        d *= 2
    carry = h_sc[...]
    groups = []
    for j in range(ngroups):
        hj = a[j] * carry + b[j]
        groups.append(hj)
        carry = jnp.broadcast_to(hj[SUBLANES - 1:SUBLANES], hj.shape)
    h = jnp.concatenate(groups, axis=0)
    h_sc[...] = carry
    hlast_ref[...] = carry
    o_ref[...] = (h * _gelu_tanh(gate_ref[...])).astype(o_ref.dtype)


def _rg_lru(xg, tail0, h0, conv_w, conv_b, w_r, b_r, w_i, b_i, lru_lambda, *, batch, seq, tt, hc):
    width = conv_w.shape[1]
    cw = hc * LANES
    ngrp = width // cw
    nt = seq // tt
    row = lambda a: a.reshape(1, width).astype(F32)
    chan = lambda r: pl.BlockSpec((r, cw), lambda b, g, t: (0, g))
    wspec = pl.BlockSpec((hc, LANES, LANES), lambda b, g, t: (g, 0, 0))
    kern = functools.partial(_lru_kernel, tt=tt, hc=hc)
    return pl.pallas_call(
        kern,
        grid=(batch, ngrp, nt),
        in_specs=[
            pl.BlockSpec((tt, cw), lambda b, g, t: (b * nt + t, g)),
            pl.BlockSpec((tt, cw), lambda b, g, t: (b * nt + t, ngrp + g)),
            chan(SUBLANES), chan(SUBLANES),
            chan(CONV_WIDTH), chan(1),
            wspec, chan(1), wspec, chan(1), chan(1),
        ],
        out_specs=[pl.BlockSpec((tt, cw), lambda b, g, t: (b * nt + t, g)),
                   pl.BlockSpec((SUBLANES, cw), lambda b, g, t: (b, g))],
        out_shape=[jax.ShapeDtypeStruct((batch * seq, width), BF16),
                   jax.ShapeDtypeStruct((batch * SUBLANES, width), F32)],
        scratch_shapes=[pltpu.VMEM((tt + SUBLANES, cw), F32),
                        pltpu.VMEM((SUBLANES, cw), F32)],
        compiler_params=_params(3),
        name="rg_lru",
    )(xg, xg, tail0, h0, conv_w.astype(F32), row(conv_b), w_r, row(b_r), w_i, row(b_i),
      row(lru_lambda))


def _accumulate_k_tile(a, w_ref, acc, rows, first=False):
    a = a[:, :rows]
    tc = acc.shape[2]
    for e in range(acc.shape[0]):
        w = w_ref[0:rows, e * tc:(e + 1) * tc].astype(BF16)
        d = jnp.dot(a, w, preferred_element_type=F32)
        acc[e] = d if first else acc[e] + d


def _row_rms_scale(acc, n):
    ssq = jnp.sum(acc[0] * acc[0], axis=1, keepdims=True)
    for e in range(1, acc.shape[0]):
        ssq = ssq + jnp.sum(acc[e] * acc[e], axis=1, keepdims=True)
    return jnp.broadcast_to(lax.rsqrt(ssq / n + RMS_EPS), (acc.shape[1], LANES))


def _lane_tile(x, width):
    return jnp.concatenate([x] * (width // x.shape[1]), axis=1)


def _out_proj_kernel(attn_ref, rec_ref, w_ref, x_ref, g1_ref, g2_ref, h1_ref, u2_ref, acc, rs_sc,
                     ssq_sc, *, nk, ne):
    s = pl.program_id(1)
    tk = attn_ref.shape[1]
    tc = acc.shape[2]
    n = acc.shape[0] * tc

    @pl.when(s == 0)
    def _():
        _accumulate_k_tile(attn_ref[...], w_ref, acc, tk, first=True)

    @pl.when((s > 0) & (s < nk // 2))
    def _():
        _accumulate_k_tile(attn_ref[...], w_ref, acc, tk)

    @pl.when((s >= nk // 2) & (s < nk))
    def _():
        _accumulate_k_tile(rec_ref[...], w_ref, acc, tk)

    @pl.when(s == nk)
    def _():
        rs_sc[...] = _row_rms_scale(acc, n)
        ssq_sc[...] = jnp.zeros_like(ssq_sc)

    @pl.when((s >= nk) & (s < nk + ne))
    def _():
        c = s - nk
        h1 = x_ref[...] + acc[c] * _lane_tile(rs_sc[...], tc) * g1_ref[...]
        h1_ref[...] = h1
        acc[c] = h1
        ssq_sc[...] += jnp.broadcast_to(jnp.sum(h1 * h1, axis=1, keepdims=True), ssq_sc.shape)

    @pl.when(s == nk + ne)
    def _():
        rs_sc[...] = lax.rsqrt(ssq_sc[...] / n + RMS_EPS)

    @pl.when(s >= nk + ne)
    def _():
        c = s - nk - ne
        u2_ref[...] = (acc[c] * _lane_tile(rs_sc[...], tc) * g2_ref[...]).astype(u2_ref.dtype)


def _out_proj(attn, rec, w_out, x2d, g_post, g_pre, *, tm, tk, tc):
    m, k_half = attn.shape
    n = w_out.shape[1]
    assert rec.shape == attn.shape and w_out.shape[0] == 2 * k_half
    assert m % tm == 0 and k_half % tk == 0 and n % tc == 0
    nkh = k_half // tk
    nk = 2 * nkh
    ne = n // tc
    chunk1 = lambda s: jnp.clip(s - nk, 0, ne - 1)
    chunk2 = lambda s: jnp.clip(s - nk - ne, 0, ne - 1)
    kern = functools.partial(_out_proj_kernel, nk=nk, ne=ne)
    return pl.pallas_call(
        kern,
        grid=(m // tm, nk + 2 * ne),
        in_specs=[pl.BlockSpec((tm, tk), lambda i, s: (i, jnp.minimum(s, nkh - 1))),
                  pl.BlockSpec((tm, tk), lambda i, s: (i, jnp.clip(s - nkh, 0, nkh - 1))),
                  pl.BlockSpec((tk, n), lambda i, s: (jnp.minimum(s, nk - 1), 0)),
                  pl.BlockSpec((tm, tc), lambda i, s: (i, chunk1(s))),
                  pl.BlockSpec((1, tc), lambda i, s: (0, chunk1(s))),
                  pl.BlockSpec((1, tc), lambda i, s: (0, chunk2(s)))],
        out_specs=[pl.BlockSpec((tm, tc), lambda i, s: (i, chunk1(s))),
                   pl.BlockSpec((tm, tc), lambda i, s: (i, chunk2(s)))],
        out_shape=[jax.ShapeDtypeStruct((m, n), F32), jax.ShapeDtypeStruct((m, n), BF16)],
        scratch_shapes=[pltpu.VMEM((ne, tm, tc), F32), pltpu.VMEM((tm, LANES), F32),
                        pltpu.VMEM((tm, LANES), F32)],
        compiler_params=_params(2),
        name="out_proj",
    )(attn, rec, w_out, x2d, g_post.reshape(1, n), g_pre.reshape(1, n))


def _ffn_up_kernel(a_ref, wg_ref, wu_ref, o_ref, wgb, wub):
    @pl.when(pl.program_id(1) == 0)
    def _():
        _cast_into(wgb, wg_ref)
        _cast_into(wub, wu_ref)

    a = a_ref[...]
    g = jnp.dot(a, wgb[...], preferred_element_type=F32)
    u = jnp.dot(a, wub[...], preferred_element_type=F32)
    o_ref[...] = (g * jax.nn.sigmoid(g) * u).astype(o_ref.dtype)


def _ffn_up(u2, w_gate, w_up, *, tm, tf):
    m, k = u2.shape
    f = w_gate.shape[1]
    wspec = pl.BlockSpec((k, tf), lambda j, i: (0, j))
    return pl.pallas_call(
        _ffn_up_kernel,
        grid=(f // tf, m // tm),
        in_specs=[pl.BlockSpec((tm, k), lambda j, i: (i, 0)), wspec, wspec],
        out_specs=pl.BlockSpec((tm, tf), lambda j, i: (i, j)),
        out_shape=jax.ShapeDtypeStruct((m, f), BF16),
        scratch_shapes=[pltpu.VMEM((k, tf), BF16), pltpu.VMEM((k, tf), BF16)],
        compiler_params=_params(2),
        name="ffn_up",
    )(u2, w_gate, w_up)


def _ffn_down_kernel(a_ref, w_ref, h1_ref, g_ref, o_ref, acc, rs_sc, *, nk, k_last):
    s = pl.program_id(1)
    tk = a_ref.shape[1]
    n = acc.shape[0] * acc.shape[2]

    @pl.when(s == 0)
    def _():
        _accumulate_k_tile(a_ref[...], w_ref, acc, tk, first=True)

    @pl.when((s > 0) & (s < nk - 1))
    def _():
        _accumulate_k_tile(a_ref[...], w_ref, acc, tk)

    @pl.when(s == nk - 1)
    def _():
        _accumulate_k_tile(a_ref[...], w_ref, acc, k_last)

    @pl.when(s == nk)
    def _():
        rs_sc[...] = _row_rms_scale(acc, n)

    @pl.when(s >= nk)
    def _():
        y = acc[s - nk] * _lane_tile(rs_sc[...], acc.shape[2]) * g_ref[...]
        o_ref[...] = h1_ref[...] + y


def _ffn_down(act, w_down, h1, g, *, tm, tk, tc):
    m, f = act.shape
    n = w_down.shape[1]
    nk = pl.cdiv(f, tk)
    k_last = f - (nk - 1) * tk
    ne = n // tc
    assert m % tm == 0 and n % tc == 0 and k_last % LANES == 0
    kstep = lambda s: jnp.minimum(s, nk - 1)
    chunk = lambda s: jnp.clip(s - nk, 0, ne - 1)
    kern = functools.partial(_ffn_down_kernel, nk=nk, k_last=k_last)
    return pl.pallas_call(
        kern,
        grid=(m // tm, nk + ne),
        in_specs=[pl.BlockSpec((tm, tk), lambda i, s: (i, kstep(s))),
                  pl.BlockSpec((tk, n), lambda i, s: (kstep(s), 0)),
                  pl.BlockSpec((tm, tc), lambda i, s: (i, chunk(s))),
                  pl.BlockSpec((1, tc), lambda i, s: (0, chunk(s)))],
        out_specs=pl.BlockSpec((tm, tc), lambda i, s: (i, chunk(s))),
        out_shape=jax.ShapeDtypeStruct((m, n), F32),
        scratch_shapes=[pltpu.VMEM((ne, tm, tc), F32), pltpu.VMEM((tm, LANES), F32)],
        compiler_params=_params(2),
        name="ffn_down",
    )(act, w_down, h1, g.reshape(1, n))


def _rope_tables(n_pos):
    inv_freq = 1.0 / (ROPE_THETA ** (jnp.arange(0, DIFF_QK_DIM, 2, dtype=F32) / DIFF_QK_DIM))
    ang = jnp.arange(n_pos, dtype=F32)[:, None] * inv_freq[None, :]
    cos, sin = jnp.cos(ang), jnp.sin(ang)
    return jnp.concatenate([cos, cos], axis=-1), jnp.concatenate([-sin, sin], axis=-1)


def kernel(x, meta_tokens, mix_pre_g, w_in, lambda_q1, lambda_k1, lambda_q2, lambda_k2, subln_g,
           conv_w, conv_b, w_r, b_r, w_i, b_i, lru_lambda, w_out, mix_post_g, ffn_pre_g, w_gate,
           w_up, w_down, ffn_post_g):
    batch, seq, d = x.shape
    n_meta = meta_tokens.shape[0]
    depth = w_in.shape[0]
    attn_w = d // 2
    lru_w = d - attn_w
    heads = attn_w // DIFF_V_DIM
    rows = batch * seq
    assert depth == 1, "the meta-prefix factorisation below is written for a single layer"
    assert n_meta == 2 * SUBLANES and seq % 1024 == 0

    l = 0
    lambda_init = 0.8 - 0.6 * math.exp(-0.3 * l)
    scale = DIFF_QK_DIM ** -0.5 * math.log2(math.e)
    cos_all, sin_all = _rope_tables(n_meta + seq)
    cos_x, sin_x = cos_all[n_meta:], sin_all[n_meta:]
    cos_m, sin_m = cos_all[:n_meta], sin_all[:n_meta]

    x2d = x.reshape(rows, d)
    w_in_l = w_in[l]

    tm, tn = 1024, 512
    seq_tiles = seq // tm
    rope_x = functools.partial(_rope_epilogue, n_scaled=attn_w // tn, scale=scale)
    rope_m = functools.partial(_rope_epilogue, n_scaled=0, scale=scale)
    tab_x = pl.BlockSpec((tm, DIFF_QK_DIM), lambda j, i: (i % seq_tiles, 0))
    tab_m = pl.BlockSpec((n_meta, DIFF_QK_DIM), lambda j, i: (0, 0))

    u_x = _prenorm(x2d, mix_pre_g[l], 256)
    u_m = _prenorm(meta_tokens.astype(x.dtype), mix_pre_g[l], n_meta)

    qk_x = _matmul_ws(u_x, w_in_l, 0, 2 * attn_w, tm=tm, tn=tn, out_dtype=BF16, epilogue=rope_x,
                      extras=(cos_x, sin_x), extra_specs=(tab_x, tab_x), name="in_proj_qk")
    v_x = _matmul_ws(u_x, w_in_l, 2 * attn_w, attn_w, tm=tm, tn=tn, out_dtype=BF16,
                     name="in_proj_v")
    xg_x = _matmul_ws(u_x, w_in_l, 3 * attn_w, 2 * lru_w, tm=tm, tn=tn, out_dtype=F32,
                      name="in_proj_lru")

    k_m = _matmul_ws(u_m, w_in_l, attn_w, attn_w, tm=n_meta, tn=tn, out_dtype=BF16,
                     epilogue=rope_m, extras=(cos_m, sin_m), extra_specs=(tab_m, tab_m),
                     name="in_proj_k_meta")
    v_m = _matmul_ws(u_m, w_in_l, 2 * attn_w, attn_w, tm=n_meta, tn=tn, out_dtype=BF16,
                     name="in_proj_v_meta")
    xg_m = _matmul_ws(u_m, w_in_l, 3 * attn_w, 2 * lru_w, tm=n_meta, tn=tn, out_dtype=F32,
                      name="in_proj_lru_meta")

    attn = _diff_attention(qk_x, v_x, k_m, v_m, lambda_q1[l], lambda_k1[l], lambda_q2[l],
                           lambda_k2[l], subln_g[l], batch=batch, seq=seq, heads=heads,
                           lambda_init=lambda_init, tq=512, tk=512)

    lru_args = (conv_w[l], conv_b[l], w_r[l], b_r[l], w_i[l], b_i[l], lru_lambda[l])
    zeros8 = jnp.zeros((SUBLANES, lru_w), F32)
    _, h_meta = _rg_lru(xg_m, zeros8, zeros8, *lru_args, batch=1, seq=n_meta, tt=n_meta, hc=4)
    tail_meta = xg_m[n_meta - SUBLANES:, :lru_w]
    rec, _ = _rg_lru(xg_x, tail_meta, h_meta, *lru_args, batch=batch, seq=seq, tt=512, hc=4)

    h1, u2 = _out_proj(attn, rec, w_out[l], x2d, mix_post_g[l], ffn_pre_g[l], tm=tm, tk=512, tc=512)

    act = _ffn_up(u2, w_gate[l], w_up[l], tm=tm, tf=256)
    out = _ffn_down(act, w_down[l], h1, ffn_post_g[l], tm=tm, tk=512, tc=512)
    return out.reshape(batch, seq, d)
```

```python
import functools
import math

import jax
import jax.numpy as jnp
from jax import lax
from jax.experimental import pallas as pl
from jax.experimental.pallas import tpu as pltpu

DIFF_QK_DIM = 128
DIFF_V_DIM = 2 * DIFF_QK_DIM
LRU_HEADS = 16
CONV_WIDTH = 4
LRU_C = 8.0
ROPE_THETA = 10000.0
RMS_EPS = 1e-6

LANES = 128
SUBLANES = 8
VMEM_LIMIT_BYTES = 56 * 1024 * 1024

BF16 = jnp.bfloat16
F32 = jnp.float32


def _params(n_axes):
    return pltpu.CompilerParams(
        dimension_semantics=("arbitrary",) * n_axes,
        vmem_limit_bytes=VMEM_LIMIT_BYTES,
    )


def _rms_scale(x):
    return lax.rsqrt(jnp.mean(x * x, axis=-1, keepdims=True) + RMS_EPS)


def _cast_into(dst_ref, src_ref, chunk=256, rows=None):
    rows = src_ref.shape[0] if rows is None else rows
    chunk = min(chunk, rows)
    assert rows % chunk == 0

    def body(c, carry):
        r0 = pl.multiple_of(c * chunk, chunk)
        dst_ref[pl.ds(r0, chunk), :] = src_ref[pl.ds(r0, chunk), :].astype(dst_ref.dtype)
        return carry

    lax.fori_loop(0, rows // chunk, body, 0)


def _prenorm_kernel(x_ref, g_ref, o_ref):
    x = x_ref[...]
    o_ref[...] = (x * _rms_scale(x) * g_ref[...]).astype(o_ref.dtype)


def _prenorm(x2d, g, tm):
    rows, d = x2d.shape
    return pl.pallas_call(
        _prenorm_kernel,
        grid=(rows // tm,),
        in_specs=[pl.BlockSpec((tm, d), lambda i: (i, 0)),
                  pl.BlockSpec((1, d), lambda i: (0, 0))],
        out_specs=pl.BlockSpec((tm, d), lambda i: (i, 0)),
        out_shape=jax.ShapeDtypeStruct((rows, d), BF16),
        compiler_params=_params(1),
        name="prenorm",
    )(x2d, g.reshape(1, d))


def _mm_kernel(a_ref, w_ref, *rest, epilogue, n_extra):
    extra = rest[:n_extra]
    o_ref = rest[n_extra]
    wbf = rest[n_extra + 1]

    @pl.when(pl.program_id(1) == 0)
    def _():
        _cast_into(wbf, w_ref)

    acc = jnp.dot(a_ref[...], wbf[...], preferred_element_type=F32)
    epilogue(acc, o_ref, *extra)


def _plain_epilogue(acc, o_ref):
    o_ref[...] = acc.astype(o_ref.dtype)


def _rope_epilogue(acc, o_ref, cos_ref, sin_ref, *, n_scaled, scale):
    j = pl.program_id(0)
    mult = jnp.where(j < n_scaled, jnp.float32(scale), jnp.float32(1.0))
    cos = cos_ref[...] * mult
    sin = sin_ref[...] * mult
    for c in range(acc.shape[1] // DIFF_QK_DIM):
        sl = slice(c * DIFF_QK_DIM, (c + 1) * DIFF_QK_DIM)
        y = acc[:, sl]
        o_ref[:, sl] = (y * cos + pltpu.roll(y, DIFF_QK_DIM // 2, 1) * sin).astype(o_ref.dtype)


def _matmul_ws(a, w, col0, ncols, *, tm, tn, out_dtype, epilogue=_plain_epilogue,
               extras=(), extra_specs=(), name):
    m, k = a.shape
    assert w.shape[0] == k and m % tm == 0 and ncols % tn == 0 and col0 % tn == 0
    jb0 = col0 // tn
    kern = functools.partial(_mm_kernel, epilogue=epilogue, n_extra=len(extras))
    return pl.pallas_call(
        kern,
        grid=(ncols // tn, m // tm),
        in_specs=[pl.BlockSpec((tm, k), lambda j, i: (i, 0)),
                  pl.BlockSpec((k, tn), lambda j, i: (0, jb0 + j)),
                  *extra_specs],
        out_specs=pl.BlockSpec((tm, tn), lambda j, i: (i, j)),
        out_shape=jax.ShapeDtypeStruct((m, ncols), out_dtype),
        scratch_shapes=[pltpu.VMEM((k, tn), BF16)],
        compiler_params=_params(2),
        name=name,
    )(a, w, *extras)


def _dot_nt(a, b):
    return lax.dot_general(a, b, (((1,), (1,)), ((), ())), preferred_element_type=F32)


def _attn_kernel(lq1_ref, lk1_ref, lq2_ref, lk2_ref, g_ref, q_ref, k_ref, v_ref, km_ref, vm_ref,
                 o_ref, acc_sc, m_sc, l_sc, *, tq, tk, n_meta, lambda_init):
    assert tq % tk == 0
    qi = pl.program_id(2)
    dk, dv = DIFF_QK_DIM, DIFF_V_DIM
    n_meta_pad = km_ref.shape[0]
    streams = [(2 * hh + c, slice(hh * dv + c * dk, hh * dv + (c + 1) * dk),
                slice(hh * dv, (hh + 1) * dv))
               for hh in range(q_ref.shape[1] // dv) for c in range(2)]

    def update(c, rows, blocks, first):
        nrows = blocks[0][0].shape[0]
        groups = [s[:, g * LANES:(g + 1) * LANES] for s, _ in blocks
                  for g in range(s.shape[1] // LANES)]
        mx = groups[0]
        for grp in groups[1:]:
            mx = jnp.maximum(mx, grp)
        m_cur = jnp.max(mx, axis=1, keepdims=True)
        if first:
            m_new = jnp.broadcast_to(m_cur, (nrows, LANES))
        else:
            m_prev = m_sc[c, rows]
            m_new = jnp.maximum(m_prev, m_cur)
            alpha = jnp.exp2(m_prev - m_new)
        ps = [jnp.exp2(grp - m_new) for grp in groups]
        lsum = ps[0]
        for p in ps[1:]:
            lsum = lsum + p
        pv = None
        g0 = 0
        for s, v in blocks:
            ng = s.shape[1] // LANES
            p = jnp.concatenate([p.astype(BF16) for p in ps[g0:g0 + ng]], axis=1)
            g0 += ng
            d = jnp.dot(p, v, preferred_element_type=F32)
            pv = d if pv is None else pv + d
        if first:
            l_sc[c, rows] = lsum
            acc_sc[c, rows] = pv
        else:
            l_sc[c, rows] = alpha * l_sc[c, rows] + lsum
            acc_sc[c, rows] = (jnp.concatenate([alpha] * (DIFF_V_DIM // LANES), axis=1)
                               * acc_sc[c, rows] + pv)
        m_sc[c, rows] = m_new

    nsub = tq // tk
    meta_ok = lax.broadcasted_iota(jnp.int32, (tk, n_meta_pad), 1) < n_meta
    causal = (lax.broadcasted_iota(jnp.int32, (tk, tk), 1)
              <= lax.broadcasted_iota(jnp.int32, (tk, tk), 0))
    for r in range(nsub):
        rows = slice(r * tk, (r + 1) * tk)
        for c, cs, vs in streams:
            q = q_ref[rows, cs]
            blocks = [(jnp.where(meta_ok, _dot_nt(q, km_ref[:, cs]), -jnp.inf), vm_ref[:, vs])]
            for j in range(r + 1):
                ks = pl.multiple_of((qi * nsub + j) * tk, tk)
                s = _dot_nt(q, k_ref[pl.ds(ks, tk), cs])
                if j == r:
                    s = jnp.where(causal, s, -jnp.inf)
                blocks.append((s, v_ref[pl.ds(ks, tk), vs]))
            update(c, rows, blocks, True)

    def tile(ks):
        for c, cs, vs in streams:
            s = _dot_nt(q_ref[:, cs], k_ref[pl.ds(ks, tk), cs])
            update(c, slice(0, tq), [(s, v_ref[pl.ds(ks, tk), vs])], False)

    n_full = qi * nsub

    def body(pair, carry):
        tile(pl.multiple_of(2 * pair * tk, tk))
        tile(pl.multiple_of((2 * pair + 1) * tk, tk))
        return carry

    lax.fori_loop(0, n_full // 2, body, 0)

    if nsub % 2 == 1:
        @pl.when(n_full % 2 == 1)
        def _():
            tile(pl.multiple_of((n_full - 1) * tk, tk))

    lam = (jnp.exp(jnp.sum(lq1_ref[...] * lk1_ref[...], axis=1, keepdims=True))
           - jnp.exp(jnp.sum(lq2_ref[...] * lk2_ref[...], axis=1, keepdims=True))
           + lambda_init)
    for c0, _, vs in streams[::2]:
        l0 = jnp.sum(l_sc[c0], axis=1, keepdims=True)
        l1 = jnp.sum(l_sc[c0 + 1], axis=1, keepdims=True)
        o = acc_sc[c0] / l0 - lam * (acc_sc[c0 + 1] / l1)
        o = o * _rms_scale(o) * g_ref[...] * (1.0 - lambda_init)
        o_ref[:, vs] = o.astype(o_ref.dtype)


def _diff_attention(qk, v, k_meta, v_meta, lq1, lk1, lq2, lk2, subln_g, *, batch, seq, heads,
                    lambda_init, tq, tk, hps):
    assert heads % hps == 0
    dv = DIFF_V_DIM
    bw = hps * dv
    hgroups = heads // hps
    nq = seq // tq
    n_meta = k_meta.shape[0]
    n_meta_pad = LANES
    assert n_meta <= n_meta_pad
    k_meta = jnp.pad(k_meta, ((0, n_meta_pad - n_meta), (0, 0)))
    v_meta = jnp.pad(v_meta, ((0, n_meta_pad - n_meta), (0, 0)))
    vec = lambda a: a.reshape(1, -1).astype(F32)
    small = lambda n: pl.BlockSpec((1, n), lambda b, h, i: (0, 0))
    kern = functools.partial(_attn_kernel, tq=tq, tk=tk, n_meta=n_meta, lambda_init=lambda_init)
    return pl.pallas_call(
        kern,
        grid=(batch, hgroups, nq),
        in_specs=[small(DIFF_QK_DIM)] * 4 + [
            small(dv),
            pl.BlockSpec((tq, bw), lambda b, h, i: (b * nq + i, h)),
            pl.BlockSpec((seq, bw), lambda b, h, i: (b, hgroups + h)),
            pl.BlockSpec((seq, bw), lambda b, h, i: (b, h)),
            pl.BlockSpec((n_meta_pad, bw), lambda b, h, i: (0, h)),
            pl.BlockSpec((n_meta_pad, bw), lambda b, h, i: (0, h)),
        ],
        out_specs=pl.BlockSpec((tq, bw), lambda b, h, i: (b * nq + i, h)),
        out_shape=jax.ShapeDtypeStruct((batch * seq, heads * dv), BF16),
        scratch_shapes=[pltpu.VMEM((2 * hps, tq, dv), F32),
                        pltpu.VMEM((2 * hps, tq, LANES), F32),
                        pltpu.VMEM((2 * hps, tq, LANES), F32)],
        compiler_params=_params(3),
        name="diff_attention",
    )(vec(lq1), vec(lk1), vec(lq2), vec(lk2), vec(subln_g), qk, qk, v, k_meta, v_meta)


def _gelu_tanh(x):
    c = math.sqrt(2.0 / math.pi)
    return 0.5 * x * (1.0 + jnp.tanh(c * (x + 0.044715 * (x * x * x))))


def _sigmoid(x):
    return 0.5 * jnp.tanh(0.5 * x) + 0.5


def _lru_kernel(xr_ref, gate_ref, tail0_ref, h0_ref, cw_ref, cb_ref, wr_ref, br_ref, wi_ref,
                bi_ref, lam_ref, o_ref, hlast_ref, xs_sc, h_sc, *, tt, hc):
    t = pl.program_id(2)
    blk = LANES
    halo = SUBLANES

    @pl.when(t == 0)
    def _():
        xs_sc[0:halo, :] = tail0_ref[...]
        h_sc[...] = h0_ref[...]

    xs_sc[halo:halo + tt, :] = xr_ref[...]
    xc = cb_ref[...]
    for w in range(CONV_WIDTH):
        start = halo - (CONV_WIDTH - 1) + w
        xc = xc + cw_ref[w:w + 1, :] * xs_sc[pl.ds(start, tt), :]
    xs_sc[0:halo, :] = xr_ref[tt - halo:tt, :]

    r_parts, i_parts = [], []
    for hh in range(hc):
        xh = xc[:, hh * blk:(hh + 1) * blk].astype(BF16)
        r_parts.append(jnp.dot(xh, wr_ref[hh].astype(BF16), preferred_element_type=F32))
        i_parts.append(jnp.dot(xh, wi_ref[hh].astype(BF16), preferred_element_type=F32))
    r = _sigmoid(jnp.concatenate(r_parts, axis=1) + br_ref[...])
    ig = _sigmoid(jnp.concatenate(i_parts, axis=1) + bi_ref[...])

    neg_lam = -lam_ref[...]
    softplus = jnp.maximum(neg_lam, 0.0) + jnp.log1p(jnp.exp(-jnp.abs(neg_lam)))
    log_a = (-LRU_C) * r * softplus
    a = jnp.exp(log_a)
    mult = jnp.sqrt(-jnp.tanh(log_a) * (a * a + 1.0))
    b = mult * (ig * xc)

    ngroups = tt // SUBLANES
    a = a.reshape(ngroups, SUBLANES, a.shape[1])
    b = b.reshape(a.shape)
    row = lax.broadcasted_iota(jnp.int32, (1,) + a.shape[1:], 1)
    d = 1
    while d < SUBLANES:
        keep = row >= d
        a_sh = jnp.where(keep, pltpu.roll(a, d, 1), 1.0)
        b_sh = jnp.where(keep, pltpu.roll(b, d, 1), 0.0)
        b = a * b_sh + b
        a = a * a_sh
        d *= 2
    carry = h_sc[...]
    groups = []
    for j in range(ngroups):
        hj = a[j] * carry + b[j]
        groups.append(hj)
        carry = jnp.broadcast_to(hj[SUBLANES - 1:SUBLANES], hj.shape)
    h = jnp.concatenate(groups, axis=0)
    h_sc[...] = carry
    hlast_ref[...] = carry
    o_ref[...] = (h * _gelu_tanh(gate_ref[...])).astype(o_ref.dtype)


def _rg_lru(xg, tail0, h0, conv_w, conv_b, w_r, b_r, w_i, b_i, lru_lambda, *, batch, seq, tt, hc):
    width = conv_w.shape[1]
    cw = hc * LANES
    ngrp = width // cw
    nt = seq // tt
    row = lambda a: a.reshape(1, width).astype(F32)
    chan = lambda r: pl.BlockSpec((r, cw), lambda b, g, t: (0, g))
    wspec = pl.BlockSpec((hc, LANES, LANES), lambda b, g, t: (g, 0, 0))
    kern = functools.partial(_lru_kernel, tt=tt, hc=hc)
    return pl.pallas_call(
        kern,
        grid=(batch, ngrp, nt),
        in_specs=[
            pl.BlockSpec((tt, cw), lambda b, g, t: (b * nt + t, g)),
            pl.BlockSpec((tt, cw), lambda b, g, t: (b * nt + t, ngrp + g)),
            chan(SUBLANES), chan(SUBLANES),
            chan(CONV_WIDTH), chan(1),
            wspec, chan(1), wspec, chan(1), chan(1),
        ],
        out_specs=[pl.BlockSpec((tt, cw), lambda b, g, t: (b * nt + t, g)),
                   pl.BlockSpec((SUBLANES, cw), lambda b, g, t: (b, g))],
        out_shape=[jax.ShapeDtypeStruct((batch * seq, width), BF16),
                   jax.ShapeDtypeStruct((batch * SUBLANES, width), F32)],
        scratch_shapes=[pltpu.VMEM((tt + SUBLANES, cw), F32),
                        pltpu.VMEM((SUBLANES, cw), F32)],
        compiler_params=_params(3),
        name="rg_lru",
    )(xg, xg, tail0, h0, conv_w.astype(F32), row(conv_b), w_r, row(b_r), w_i, row(b_i),
      row(lru_lambda))


def _accumulate_k_tile(a, w_ref, acc, rows, first=False):
    a = a[:, :rows]
    tc = acc.shape[2]
    for e in range(acc.shape[0]):
        w = w_ref[0:rows, e * tc:(e + 1) * tc].astype(BF16)
        d = jnp.dot(a, w, preferred_element_type=F32)
        acc[e] = d if first else acc[e] + d


def _row_rms_scale(acc, n):
    ssq = jnp.sum(acc[0] * acc[0], axis=1, keepdims=True)
    for e in range(1, acc.shape[0]):
        ssq = ssq + jnp.sum(acc[e] * acc[e], axis=1, keepdims=True)
    return jnp.broadcast_to(lax.rsqrt(ssq / n + RMS_EPS), (acc.shape[1], LANES))


def _lane_tile(x, width):
    return jnp.concatenate([x] * (width // x.shape[1]), axis=1)


def _out_proj_kernel(attn_ref, rec_ref, w_ref, x_ref, g1_ref, g2_ref, h1_ref, u2_ref, rs2_ref, acc,
                     rs_sc, ssq_sc, *, nk, ne):
    s = pl.program_id(1)
    tk = attn_ref.shape[1]
    tc = acc.shape[2]
    n = acc.shape[0] * tc

    @pl.when(s == 0)
    def _():
        _accumulate_k_tile(attn_ref[...], w_ref, acc, tk, first=True)

    @pl.when((s > 0) & (s < nk // 2))
    def _():
        _accumulate_k_tile(attn_ref[...], w_ref, acc, tk)

    @pl.when((s >= nk // 2) & (s < nk))
    def _():
        _accumulate_k_tile(rec_ref[...], w_ref, acc, tk)

    @pl.when(s == nk)
    def _():
        rs_sc[...] = _row_rms_scale(acc, n)
        ssq_sc[...] = jnp.zeros_like(ssq_sc)

    @pl.when(s >= nk)
    def _():
        h1 = x_ref[...] + acc[s - nk] * _lane_tile(rs_sc[...], tc) * g1_ref[...]
        h1_ref[...] = h1
        u2_ref[...] = (h1 * g2_ref[...]).astype(u2_ref.dtype)
        ssq_sc[...] += jnp.broadcast_to(jnp.sum(h1 * h1, axis=1, keepdims=True), ssq_sc.shape)

    @pl.when(s == nk + ne - 1)
    def _():
        rs2_ref[...] = lax.rsqrt(ssq_sc[...] / n + RMS_EPS)


def _out_proj(attn, rec, w_out, x2d, g_post, g_pre, *, tm, tk, tc):
    m, k_half = attn.shape
    n = w_out.shape[1]
    assert rec.shape == attn.shape and w_out.shape[0] == 2 * k_half
    assert m % tm == 0 and k_half % tk == 0 and n % tc == 0
    nkh = k_half // tk
    nk = 2 * nkh
    ne = n // tc
    chunk = lambda s: jnp.clip(s - nk, 0, ne - 1)
    kern = functools.partial(_out_proj_kernel, nk=nk, ne=ne)
    return pl.pallas_call(
        kern,
        grid=(m // tm, nk + ne),
        in_specs=[pl.BlockSpec((tm, tk), lambda i, s: (i, jnp.minimum(s, nkh - 1))),
                  pl.BlockSpec((tm, tk), lambda i, s: (i, jnp.clip(s - nkh, 0, nkh - 1))),
                  pl.BlockSpec((tk, n), lambda i, s: (jnp.minimum(s, nk - 1), 0)),
                  pl.BlockSpec((tm, tc), lambda i, s: (i, chunk(s))),
                  pl.BlockSpec((1, tc), lambda i, s: (0, chunk(s))),
                  pl.BlockSpec((1, tc), lambda i, s: (0, chunk(s)))],
        out_specs=[pl.BlockSpec((tm, tc), lambda i, s: (i, chunk(s))),
                   pl.BlockSpec((tm, tc), lambda i, s: (i, chunk(s))),
                   pl.BlockSpec((tm, LANES), lambda i, s: (i, 0))],
        out_shape=[jax.ShapeDtypeStruct((m, n), F32), jax.ShapeDtypeStruct((m, n), BF16),
                   jax.ShapeDtypeStruct((m, LANES), F32)],
        scratch_shapes=[pltpu.VMEM((ne, tm, tc), F32), pltpu.VMEM((tm, LANES), F32),
                        pltpu.VMEM((tm, LANES), F32)],
        compiler_params=_params(2),
        name="out_proj",
    )(attn, rec, w_out, x2d, g_post.reshape(1, n), g_pre.reshape(1, n))


def _ffn_up_kernel(a_ref, rs_ref, wg_ref, wu_ref, o_ref, wgb, wub):
    @pl.when(pl.program_id(1) == 0)
    def _():
        _cast_into(wgb, wg_ref)
        _cast_into(wub, wu_ref)

    a = a_ref[...]
    rs = _lane_tile(rs_ref[...], o_ref.shape[1])
    g = jnp.dot(a, wgb[...], preferred_element_type=F32) * rs
    u = jnp.dot(a, wub[...], preferred_element_type=F32) * rs
    o_ref[...] = (g * jax.nn.sigmoid(g) * u).astype(o_ref.dtype)


def _ffn_up(u2, row_scale, w_gate, w_up, *, tm, tf):
    m, k = u2.shape
    f = w_gate.shape[1]
    wspec = pl.BlockSpec((k, tf), lambda j, i: (0, j))
    return pl.pallas_call(
        _ffn_up_kernel,
        grid=(f // tf, m // tm),
        in_specs=[pl.BlockSpec((tm, k), lambda j, i: (i, 0)),
                  pl.BlockSpec((tm, LANES), lambda j, i: (i, 0)), wspec, wspec],
        out_specs=pl.BlockSpec((tm, tf), lambda j, i: (i, j)),
        out_shape=jax.ShapeDtypeStruct((m, f), BF16),
        scratch_shapes=[pltpu.VMEM((k, tf), BF16), pltpu.VMEM((k, tf), BF16)],
        compiler_params=_params(2),
        name="ffn_up",
    )(u2, row_scale, w_gate, w_up)


def _ffn_down_kernel(a_ref, w_ref, h1_ref, g_ref, o_ref, acc, rs_sc, *, nk, k_last):
    s = pl.program_id(1)
    tk = a_ref.shape[1]
    n = acc.shape[0] * acc.shape[2]

    @pl.when(s == 0)
    def _():
        _accumulate_k_tile(a_ref[...], w_ref, acc, tk, first=True)

    @pl.when((s > 0) & (s < nk - 1))
    def _():
        _accumulate_k_tile(a_ref[...], w_ref, acc, tk)

    @pl.when(s == nk - 1)
    def _():
        _accumulate_k_tile(a_ref[...], w_ref, acc, k_last)

    @pl.when(s == nk)
    def _():
        rs_sc[...] = _row_rms_scale(acc, n)

    @pl.when(s >= nk)
    def _():
        y = acc[s - nk] * _lane_tile(rs_sc[...], acc.shape[2]) * g_ref[...]
        o_ref[...] = h1_ref[...] + y


def _ffn_down(act, w_down, h1, g, *, tm, tk, tc):
    m, f = act.shape
    n = w_down.shape[1]
    nk = pl.cdiv(f, tk)
    k_last = f - (nk - 1) * tk
    ne = n // tc
    assert m % tm == 0 and n % tc == 0 and k_last % LANES == 0
    kstep = lambda s: jnp.minimum(s, nk - 1)
    chunk = lambda s: jnp.clip(s - nk, 0, ne - 1)
    kern = functools.partial(_ffn_down_kernel, nk=nk, k_last=k_last)
    return pl.pallas_call(
        kern,
        grid=(m // tm, nk + ne),
        in_specs=[pl.BlockSpec((tm, tk), lambda i, s: (i, kstep(s))),
                  pl.BlockSpec((tk, n), lambda i, s: (kstep(s), 0)),
                  pl.BlockSpec((tm, tc), lambda i, s: (i, chunk(s))),
                  pl.BlockSpec((1, tc), lambda i, s: (0, chunk(s)))],
        out_specs=pl.BlockSpec((tm, tc), lambda i, s: (i, chunk(s))),
        out_shape=jax.ShapeDtypeStruct((m, n), F32),
        scratch_shapes=[pltpu.VMEM((ne, tm, tc), F32), pltpu.VMEM((tm, LANES), F32)],
        compiler_params=_params(2),
        name="ffn_down",
    )(act, w_down, h1, g.reshape(1, n))


def _rope_tables(n_pos):
    inv_freq = 1.0 / (ROPE_THETA ** (jnp.arange(0, DIFF_QK_DIM, 2, dtype=F32) / DIFF_QK_DIM))
    ang = jnp.arange(n_pos, dtype=F32)[:, None] * inv_freq[None, :]
    cos, sin = jnp.cos(ang), jnp.sin(ang)
    return jnp.concatenate([cos, cos], axis=-1), jnp.concatenate([-sin, sin], axis=-1)


def kernel(x, meta_tokens, mix_pre_g, w_in, lambda_q1, lambda_k1, lambda_q2, lambda_k2, subln_g,
           conv_w, conv_b, w_r, b_r, w_i, b_i, lru_lambda, w_out, mix_post_g, ffn_pre_g, w_gate,
           w_up, w_down, ffn_post_g):
    batch, seq, d = x.shape
    n_meta = meta_tokens.shape[0]
    depth = w_in.shape[0]
    attn_w = d // 2
    lru_w = d - attn_w
    heads = attn_w // DIFF_V_DIM
    rows = batch * seq
    assert depth == 1, "the meta-prefix factorisation below is written for a single layer"
    assert n_meta == 2 * SUBLANES and seq % 1024 == 0

    l = 0
    lambda_init = 0.8 - 0.6 * math.exp(-0.3 * l)
    scale = DIFF_QK_DIM ** -0.5 * math.log2(math.e)
    cos_all, sin_all = _rope_tables(n_meta + seq)
    cos_x, sin_x = cos_all[n_meta:], sin_all[n_meta:]
    cos_m, sin_m = cos_all[:n_meta], sin_all[:n_meta]

    x2d = x.reshape(rows, d)
    w_in_l = w_in[l]

    tm, tn = 1024, 512
    seq_tiles = seq // tm
    rope_x = functools.partial(_rope_epilogue, n_scaled=attn_w // tn, scale=scale)
    rope_m = functools.partial(_rope_epilogue, n_scaled=0, scale=scale)
    tab_x = pl.BlockSpec((tm, DIFF_QK_DIM), lambda j, i: (i % seq_tiles, 0))
    tab_m = pl.BlockSpec((n_meta, DIFF_QK_DIM), lambda j, i: (0, 0))

    u_x = _prenorm(x2d, mix_pre_g[l], 256)
    u_m = _prenorm(meta_tokens.astype(x.dtype), mix_pre_g[l], n_meta)

    qk_x = _matmul_ws(u_x, w_in_l, 0, 2 * attn_w, tm=tm, tn=tn, out_dtype=BF16, epilogue=rope_x,
                      extras=(cos_x, sin_x), extra_specs=(tab_x, tab_x), name="in_proj_qk")
    v_x = _matmul_ws(u_x, w_in_l, 2 * attn_w, attn_w, tm=tm, tn=tn, out_dtype=BF16,
                     name="in_proj_v")
    xg_x = _matmul_ws(u_x, w_in_l, 3 * attn_w, 2 * lru_w, tm=tm, tn=tn, out_dtype=F32,
                      name="in_proj_lru")

    k_m = _matmul_ws(u_m, w_in_l, attn_w, attn_w, tm=n_meta, tn=tn, out_dtype=BF16,
                     epilogue=rope_m, extras=(cos_m, sin_m), extra_specs=(tab_m, tab_m),
                     name="in_proj_k_meta")
    v_m = _matmul_ws(u_m, w_in_l, 2 * attn_w, attn_w, tm=n_meta, tn=tn, out_dtype=BF16,
                     name="in_proj_v_meta")
    xg_m = _matmul_ws(u_m, w_in_l, 3 * attn_w, 2 * lru_w, tm=n_meta, tn=tn, out_dtype=F32,
                      name="in_proj_lru_meta")

    attn = _diff_attention(qk_x, v_x, k_m, v_m, lambda_q1[l], lambda_k1[l], lambda_q2[l],
                           lambda_k2[l], subln_g[l], batch=batch, seq=seq, heads=heads,
                           lambda_init=lambda_init, tq=1024, tk=512, hps=2)

    lru_args = (conv_w[l], conv_b[l], w_r[l], b_r[l], w_i[l], b_i[l], lru_lambda[l])
    zeros8 = jnp.zeros((SUBLANES, lru_w), F32)
    _, h_meta = _rg_lru(xg_m, zeros8, zeros8, *lru_args, batch=1, seq=n_meta, tt=n_meta, hc=4)
    tail_meta = xg_m[n_meta - SUBLANES:, :lru_w]
    rec, _ = _rg_lru(xg_x, tail_meta, h_meta, *lru_args, batch=batch, seq=seq, tt=512, hc=4)

    h1, u2, rs2 = _out_proj(attn, rec, w_out[l], x2d, mix_post_g[l], ffn_pre_g[l], tm=tm, tk=512,
                            tc=512)

    act = _ffn_up(u2, rs2, w_gate[l], w_up[l], tm=tm, tf=256)
    out = _ffn_down(act, w_down[l], h1, ffn_post_g[l], tm=tm, tk=512, tc=512)
    return out.reshape(batch, seq, d)
```

```python
import functools
import math
from typing import NamedTuple

import jax
import jax.numpy as jnp
from jax import lax
from jax.experimental import pallas as pl
from jax.experimental.pallas import tpu as pltpu

DIFF_QK_DIM = 128
DIFF_V_DIM = 2 * DIFF_QK_DIM
LRU_HEADS = 16
CONV_WIDTH = 4
LRU_C = 8.0
ROPE_THETA = 10000.0
RMS_EPS = 1e-6

LANES = 128
SUBLANES = 8
MXU_DIM = 256
VMEM_BYTES = 64 * 1024 * 1024
VMEM_LIMIT_BYTES = VMEM_BYTES * 7 // 8

BF16 = jnp.bfloat16
F32 = jnp.float32


def _params(n_axes):
    return pltpu.CompilerParams(
        dimension_semantics=("arbitrary",) * n_axes,
        vmem_limit_bytes=VMEM_LIMIT_BYTES,
    )


def _rms_scale(x):
    return lax.rsqrt(jnp.mean(x * x, axis=-1, keepdims=True) + RMS_EPS)


def _cast_into(dst_ref, src_ref, chunk=256, rows=None):
    rows = src_ref.shape[0] if rows is None else rows
    chunk = min(chunk, rows)
    assert rows % chunk == 0

    def body(c, carry):
        r0 = pl.multiple_of(c * chunk, chunk)
        dst_ref[pl.ds(r0, chunk), :] = src_ref[pl.ds(r0, chunk), :].astype(dst_ref.dtype)
        return carry

    lax.fori_loop(0, rows // chunk, body, 0)


def _prenorm_kernel(x_ref, g_ref, o_ref):
    x = x_ref[...]
    o_ref[...] = (x * _rms_scale(x) * g_ref[...]).astype(o_ref.dtype)


def _prenorm(x2d, g, tm):
    rows, d = x2d.shape
    return pl.pallas_call(
        _prenorm_kernel,
        grid=(rows // tm,),
        in_specs=[pl.BlockSpec((tm, d), lambda i: (i, 0)),
                  pl.BlockSpec((1, d), lambda i: (0, 0))],
        out_specs=pl.BlockSpec((tm, d), lambda i: (i, 0)),
        out_shape=jax.ShapeDtypeStruct((rows, d), BF16),
        compiler_params=_params(1),
        name="prenorm",
    )(x2d, g.reshape(1, d))


def _mm_kernel(a_ref, w_ref, *rest, epilogue, n_extra, row_chunks):
    extra = rest[:n_extra]
    o_ref = rest[n_extra]
    wbf = rest[n_extra + 1]

    @pl.when(pl.program_id(1) == 0)
    def _():
        _cast_into(wbf, w_ref)

    chunk = a_ref.shape[0] // row_chunks
    for c in range(row_chunks):
        rows = slice(c * chunk, (c + 1) * chunk)
        acc = jnp.dot(a_ref[rows, :], wbf[...], preferred_element_type=F32)
        epilogue(acc, o_ref, rows, *extra)


def _plain_epilogue(acc, o_ref, rows):
    o_ref[rows, :] = acc.astype(o_ref.dtype)


def _rope_epilogue(acc, o_ref, rows, cos_ref, sin_ref, *, n_scaled, scale):
    j = pl.program_id(0)
    mult = jnp.where(j < n_scaled, jnp.float32(scale), jnp.float32(1.0))
    cos = cos_ref[rows, :] * mult
    sin = sin_ref[rows, :] * mult
    for c in range(acc.shape[1] // DIFF_QK_DIM):
        sl = slice(c * DIFF_QK_DIM, (c + 1) * DIFF_QK_DIM)
        y = acc[:, sl]
        o_ref[rows, sl] = (y * cos + pltpu.roll(y, DIFF_QK_DIM // 2, 1) * sin).astype(o_ref.dtype)


def _matmul_ws(a, w, col0, ncols, *, tm, tn, out_dtype, epilogue=_plain_epilogue,
               extras=(), extra_specs=(), row_chunks=1, name):
    m, k = a.shape
    assert w.shape[0] == k and m % tm == 0 and ncols % tn == 0 and col0 % tn == 0
    jb0 = col0 // tn
    kern = functools.partial(_mm_kernel, epilogue=epilogue, n_extra=len(extras),
                             row_chunks=row_chunks)
    return pl.pallas_call(
        kern,
        grid=(ncols // tn, m // tm),
        in_specs=[pl.BlockSpec((tm, k), lambda j, i: (i, 0)),
                  pl.BlockSpec((k, tn), lambda j, i: (0, jb0 + j)),
                  *extra_specs],
        out_specs=pl.BlockSpec((tm, tn), lambda j, i: (i, j)),
        out_shape=jax.ShapeDtypeStruct((m, ncols), out_dtype),
        scratch_shapes=[pltpu.VMEM((k, tn), BF16)],
        compiler_params=_params(2),
        name=name,
    )(a, w, *extras)


def _dot_nt(a, b):
    return lax.dot_general(a, b, (((1,), (1,)), ((), ())), preferred_element_type=F32)


def _attn_kernel(lq1_ref, lk1_ref, lq2_ref, lk2_ref, g_ref, q_ref, k_ref, v_ref, km_ref, vm_ref,
                 o_ref, acc_sc, m_sc, l_sc, *, tq, tk, n_meta, lambda_init):
    assert tq % tk == 0
    qi = pl.program_id(2)
    dk, dv = DIFF_QK_DIM, DIFF_V_DIM
    n_meta_pad = km_ref.shape[0]
    streams = [(2 * hh + c, slice(hh * dv + c * dk, hh * dv + (c + 1) * dk),
                slice(hh * dv, (hh + 1) * dv))
               for hh in range(q_ref.shape[1] // dv) for c in range(2)]

    def update(c, rows, blocks, first):
        nrows = blocks[0][0].shape[0]
        groups = [s[:, g * LANES:(g + 1) * LANES] for s, _ in blocks
                  for g in range(s.shape[1] // LANES)]
        mx = groups[0]
        for grp in groups[1:]:
            mx = jnp.maximum(mx, grp)
        m_cur = jnp.max(mx, axis=1, keepdims=True)
        if first:
            m_new = jnp.broadcast_to(m_cur, (nrows, LANES))
        else:
            m_prev = m_sc[c, rows]
            m_new = jnp.maximum(m_prev, m_cur)
            alpha = jnp.exp2(m_prev - m_new)
        ps = [jnp.exp2(grp - m_new) for grp in groups]
        lsum = ps[0]
        for p in ps[1:]:
            lsum = lsum + p
        pv = None
        g0 = 0
        for s, v in blocks:
            ng = s.shape[1] // LANES
            p = jnp.concatenate([p.astype(BF16) for p in ps[g0:g0 + ng]], axis=1)
            g0 += ng
            d = jnp.dot(p, v, preferred_element_type=F32)
            pv = d if pv is None else pv + d
        if first:
            l_sc[c, rows] = lsum
            acc_sc[c, rows] = pv
        else:
            l_sc[c, rows] = alpha * l_sc[c, rows] + lsum
            acc_sc[c, rows] = (jnp.concatenate([alpha] * (DIFF_V_DIM // LANES), axis=1)
                               * acc_sc[c, rows] + pv)
        m_sc[c, rows] = m_new

    nsub = tq // tk
    meta_ok = lax.broadcasted_iota(jnp.int32, (tk, n_meta_pad), 1) < n_meta
    causal = (lax.broadcasted_iota(jnp.int32, (tk, tk), 1)
              <= lax.broadcasted_iota(jnp.int32, (tk, tk), 0))
    for r in range(nsub):
        rows = slice(r * tk, (r + 1) * tk)
        for c, cs, vs in streams:
            q = q_ref[rows, cs]
            blocks = [(jnp.where(meta_ok, _dot_nt(q, km_ref[:, cs]), -jnp.inf), vm_ref[:, vs])]
            for j in range(r + 1):
                ks = pl.multiple_of((qi * nsub + j) * tk, tk)
                s = _dot_nt(q, k_ref[pl.ds(ks, tk), cs])
                if j == r:
                    s = jnp.where(causal, s, -jnp.inf)
                blocks.append((s, v_ref[pl.ds(ks, tk), vs]))
            update(c, rows, blocks, True)

    def tile(ks):
        for c, cs, vs in streams:
            s = _dot_nt(q_ref[:, cs], k_ref[pl.ds(ks, tk), cs])
            update(c, slice(0, tq), [(s, v_ref[pl.ds(ks, tk), vs])], False)

    n_full = qi * nsub

    def body(pair, carry):
        tile(pl.multiple_of(2 * pair * tk, tk))
        tile(pl.multiple_of((2 * pair + 1) * tk, tk))
        return carry

    lax.fori_loop(0, n_full // 2, body, 0)

    if nsub % 2 == 1:
        @pl.when(n_full % 2 == 1)
        def _():
            tile(pl.multiple_of((n_full - 1) * tk, tk))

    lam = (jnp.exp(jnp.sum(lq1_ref[...] * lk1_ref[...], axis=1, keepdims=True))
           - jnp.exp(jnp.sum(lq2_ref[...] * lk2_ref[...], axis=1, keepdims=True))
           + lambda_init)
    for c0, _, vs in streams[::2]:
        l0 = jnp.sum(l_sc[c0], axis=1, keepdims=True)
        l1 = jnp.sum(l_sc[c0 + 1], axis=1, keepdims=True)
        o = acc_sc[c0] / l0 - lam * (acc_sc[c0 + 1] / l1)
        o = o * _rms_scale(o) * g_ref[...] * (1.0 - lambda_init)
        o_ref[:, vs] = o.astype(o_ref.dtype)


def _diff_attention(qk, v, k_meta, v_meta, lq1, lk1, lq2, lk2, subln_g, *, batch, seq, heads,
                    lambda_init, tq, tk, hps):
    assert heads % hps == 0
    dv = DIFF_V_DIM
    bw = hps * dv
    hgroups = heads // hps
    nq = seq // tq
    n_meta = k_meta.shape[0]
    n_meta_pad = LANES
    assert n_meta <= n_meta_pad
    k_meta = jnp.pad(k_meta, ((0, n_meta_pad - n_meta), (0, 0)))
    v_meta = jnp.pad(v_meta, ((0, n_meta_pad - n_meta), (0, 0)))
    vec = lambda a: a.reshape(1, -1).astype(F32)
    small = lambda n: pl.BlockSpec((1, n), lambda b, h, i: (0, 0))
    kern = functools.partial(_attn_kernel, tq=tq, tk=tk, n_meta=n_meta, lambda_init=lambda_init)
    return pl.pallas_call(
        kern,
        grid=(batch, hgroups, nq),
        in_specs=[small(DIFF_QK_DIM)] * 4 + [
            small(dv),
            pl.BlockSpec((tq, bw), lambda b, h, i: (b * nq + i, h)),
            pl.BlockSpec((seq, bw), lambda b, h, i: (b, hgroups + h)),
            pl.BlockSpec((seq, bw), lambda b, h, i: (b, h)),
            pl.BlockSpec((n_meta_pad, bw), lambda b, h, i: (0, h)),
            pl.BlockSpec((n_meta_pad, bw), lambda b, h, i: (0, h)),
        ],
        out_specs=pl.BlockSpec((tq, bw), lambda b, h, i: (b * nq + i, h)),
        out_shape=jax.ShapeDtypeStruct((batch * seq, heads * dv), BF16),
        scratch_shapes=[pltpu.VMEM((2 * hps, tq, dv), F32),
                        pltpu.VMEM((2 * hps, tq, LANES), F32),
                        pltpu.VMEM((2 * hps, tq, LANES), F32)],
        compiler_params=_params(3),
        name="diff_attention",
    )(vec(lq1), vec(lk1), vec(lq2), vec(lk2), vec(subln_g), qk, qk, v, k_meta, v_meta)


def _gelu_tanh(x):
    c = math.sqrt(2.0 / math.pi)
    return 0.5 * x * (1.0 + jnp.tanh(c * (x + 0.044715 * (x * x * x))))


def _sigmoid(x):
    return 0.5 * jnp.tanh(0.5 * x) + 0.5


def _lru_kernel(xr_ref, gate_ref, tail0_ref, h0_ref, cw_ref, cb_ref, wr_ref, br_ref, wi_ref,
                bi_ref, lam_ref, o_ref, hlast_ref, xs_sc, h_sc, *, tt, hc):
    t = pl.program_id(2)
    blk = LANES
    halo = SUBLANES

    @pl.when(t == 0)
    def _():
        xs_sc[0:halo, :] = tail0_ref[...]
        h_sc[...] = h0_ref[...]

    xs_sc[halo:halo + tt, :] = xr_ref[...]
    xc = cb_ref[...]
    for w in range(CONV_WIDTH):
        start = halo - (CONV_WIDTH - 1) + w
        xc = xc + cw_ref[w:w + 1, :] * xs_sc[pl.ds(start, tt), :]
    xs_sc[0:halo, :] = xr_ref[tt - halo:tt, :]

    r_parts, i_parts = [], []
    for hh in range(hc):
        xh = xc[:, hh * blk:(hh + 1) * blk].astype(BF16)
        r_parts.append(jnp.dot(xh, wr_ref[hh].astype(BF16), preferred_element_type=F32))
        i_parts.append(jnp.dot(xh, wi_ref[hh].astype(BF16), preferred_element_type=F32))
    r = _sigmoid(jnp.concatenate(r_parts, axis=1) + br_ref[...])
    ig = _sigmoid(jnp.concatenate(i_parts, axis=1) + bi_ref[...])

    neg_lam = -lam_ref[...]
    softplus = jnp.maximum(neg_lam, 0.0) + jnp.log1p(jnp.exp(-jnp.abs(neg_lam)))
    log_a = (-LRU_C) * r * softplus
    a = jnp.exp(log_a)
    mult = jnp.sqrt(-jnp.tanh(log_a) * (a * a + 1.0))
    b = mult * (ig * xc)

    ngroups = tt // SUBLANES
    a = a.reshape(ngroups, SUBLANES, a.shape[1])
    b = b.reshape(a.shape)
    row = lax.broadcasted_iota(jnp.int32, (1,) + a.shape[1:], 1)
    d = 1
    while d < SUBLANES:
        keep = row >= d
        a_sh = jnp.where(keep, pltpu.roll(a, d, 1), 1.0)
        b_sh = jnp.where(keep, pltpu.roll(b, d, 1), 0.0)
        b = a * b_sh + b
        a = a * a_sh
        d *= 2
    carry = h_sc[...]
    groups = []
    for j in range(ngroups):
        hj = a[j] * carry + b[j]
        groups.append(hj)
        carry = jnp.broadcast_to(hj[SUBLANES - 1:SUBLANES], hj.shape)
    h = jnp.concatenate(groups, axis=0)
    h_sc[...] = carry
    hlast_ref[...] = carry
    o_ref[...] = (h * _gelu_tanh(gate_ref[...])).astype(o_ref.dtype)


def _rg_lru(xg, tail0, h0, conv_w, conv_b, w_r, b_r, w_i, b_i, lru_lambda, *, batch, seq, tt, hc):
    width = conv_w.shape[1]
    cw = hc * LANES
    ngrp = width // cw
    nt = seq // tt
    row = lambda a: a.reshape(1, width).astype(F32)
    chan = lambda r: pl.BlockSpec((r, cw), lambda b, g, t: (0, g))
    wspec = pl.BlockSpec((hc, LANES, LANES), lambda b, g, t: (g, 0, 0))
    kern = functools.partial(_lru_kernel, tt=tt, hc=hc)
    return pl.pallas_call(
        kern,
        grid=(batch, ngrp, nt),
        in_specs=[
            pl.BlockSpec((tt, cw), lambda b, g, t: (b * nt + t, g)),
            pl.BlockSpec((tt, cw), lambda b, g, t: (b * nt + t, ngrp + g)),
            chan(SUBLANES), chan(SUBLANES),
            chan(CONV_WIDTH), chan(1),
            wspec, chan(1), wspec, chan(1), chan(1),
        ],
        out_specs=[pl.BlockSpec((tt, cw), lambda b, g, t: (b * nt + t, g)),
                   pl.BlockSpec((SUBLANES, cw), lambda b, g, t: (b, g))],
        out_shape=[jax.ShapeDtypeStruct((batch * seq, width), BF16),
                   jax.ShapeDtypeStruct((batch * SUBLANES, width), F32)],
        scratch_shapes=[pltpu.VMEM((tt + SUBLANES, cw), F32),
                        pltpu.VMEM((SUBLANES, cw), F32)],
        compiler_params=_params(3),
        name="rg_lru",
    )(xg, xg, tail0, h0, conv_w.astype(F32), row(conv_b), w_r, row(b_r), w_i, row(b_i),
      row(lru_lambda))


def _accumulate_k_tile(a, w_ref, acc, rows, first=False):
    a = a[:, :rows]
    tc = acc.shape[2]
    for e in range(acc.shape[0]):
        w = w_ref[0:rows, e * tc:(e + 1) * tc].astype(BF16)
        d = jnp.dot(a, w, preferred_element_type=F32)
        acc[e] = d if first else acc[e] + d


def _row_rms_scale(acc, n):
    ssq = jnp.sum(acc[0] * acc[0], axis=1, keepdims=True)
    for e in range(1, acc.shape[0]):
        ssq = ssq + jnp.sum(acc[e] * acc[e], axis=1, keepdims=True)
    return jnp.broadcast_to(lax.rsqrt(ssq / n + RMS_EPS), (acc.shape[1], LANES))


def _lane_tile(x, width):
    return jnp.concatenate([x] * (width // x.shape[1]), axis=1)


def _out_proj_kernel(attn_ref, rec_ref, w_ref, x_ref, g1_ref, g2_ref, h1_ref, u2_ref, rs2_ref, acc,
                     rs_sc, ssq_sc, *, nk, ne):
    s = pl.program_id(1)
    tk = attn_ref.shape[1]
    tc = acc.shape[2]
    n = acc.shape[0] * tc

    @pl.when(s == 0)
    def _():
        _accumulate_k_tile(attn_ref[...], w_ref, acc, tk, first=True)

    @pl.when((s > 0) & (s < nk // 2))
    def _():
        _accumulate_k_tile(attn_ref[...], w_ref, acc, tk)

    @pl.when((s >= nk // 2) & (s < nk))
    def _():
        _accumulate_k_tile(rec_ref[...], w_ref, acc, tk)

    @pl.when(s == nk)
    def _():
        rs_sc[...] = _row_rms_scale(acc, n)
        ssq_sc[...] = jnp.zeros_like(ssq_sc)

    @pl.when(s >= nk)
    def _():
        h1 = x_ref[...] + acc[s - nk] * _lane_tile(rs_sc[...], tc) * g1_ref[...]
        h1_ref[...] = h1
        u2_ref[...] = (h1 * g2_ref[...]).astype(u2_ref.dtype)
        ssq_sc[...] += jnp.broadcast_to(jnp.sum(h1 * h1, axis=1, keepdims=True), ssq_sc.shape)

    @pl.when(s == nk + ne - 1)
    def _():
        rs2_ref[...] = lax.rsqrt(ssq_sc[...] / n + RMS_EPS)


def _out_proj(attn, rec, w_out, x2d, g_post, g_pre, *, tm, tk, tc):
    m, k_half = attn.shape
    n = w_out.shape[1]
    assert rec.shape == attn.shape and w_out.shape[0] == 2 * k_half
    assert m % tm == 0 and k_half % tk == 0 and n % tc == 0
    nkh = k_half // tk
    nk = 2 * nkh
    ne = n // tc
    chunk = lambda s: jnp.clip(s - nk, 0, ne - 1)
    kern = functools.partial(_out_proj_kernel, nk=nk, ne=ne)
    return pl.pallas_call(
        kern,
        grid=(m // tm, nk + ne),
        in_specs=[pl.BlockSpec((tm, tk), lambda i, s: (i, jnp.minimum(s, nkh - 1))),
                  pl.BlockSpec((tm, tk), lambda i, s: (i, jnp.clip(s - nkh, 0, nkh - 1))),
                  pl.BlockSpec((tk, n), lambda i, s: (jnp.minimum(s, nk - 1), 0)),
                  pl.BlockSpec((tm, tc), lambda i, s: (i, chunk(s))),
                  pl.BlockSpec((1, tc), lambda i, s: (0, chunk(s))),
                  pl.BlockSpec((1, tc), lambda i, s: (0, chunk(s)))],
        out_specs=[pl.BlockSpec((tm, tc), lambda i, s: (i, chunk(s))),
                   pl.BlockSpec((tm, tc), lambda i, s: (i, chunk(s))),
                   pl.BlockSpec((tm, LANES), lambda i, s: (i, 0))],
        out_shape=[jax.ShapeDtypeStruct((m, n), F32), jax.ShapeDtypeStruct((m, n), BF16),
                   jax.ShapeDtypeStruct((m, LANES), F32)],
        scratch_shapes=[pltpu.VMEM((ne, tm, tc), F32), pltpu.VMEM((tm, LANES), F32),
                        pltpu.VMEM((tm, LANES), F32)],
        compiler_params=_params(2),
        name="out_proj",
    )(attn, rec, w_out, x2d, g_post.reshape(1, n), g_pre.reshape(1, n))


def _ffn_up_kernel(a_ref, rs_ref, wg_ref, wu_ref, o_ref, wgb, wub, *, row_chunks):
    @pl.when(pl.program_id(1) == 0)
    def _():
        _cast_into(wgb, wg_ref)
        _cast_into(wub, wu_ref)

    chunk = a_ref.shape[0] // row_chunks
    for c in range(row_chunks):
        rows = slice(c * chunk, (c + 1) * chunk)
        a = a_ref[rows, :]
        rs = _lane_tile(rs_ref[rows, :], o_ref.shape[1])
        g = jnp.dot(a, wgb[...], preferred_element_type=F32) * rs
        u = jnp.dot(a, wub[...], preferred_element_type=F32) * rs
        o_ref[rows, :] = (g * jax.nn.sigmoid(g) * u).astype(o_ref.dtype)


def _ffn_up(u2, row_scale, w_gate, w_up, *, tm, tf, row_chunks):
    m, k = u2.shape
    f = w_gate.shape[1]
    wspec = pl.BlockSpec((k, tf), lambda j, i: (0, j))
    return pl.pallas_call(
        functools.partial(_ffn_up_kernel, row_chunks=row_chunks),
        grid=(f // tf, m // tm),
        in_specs=[pl.BlockSpec((tm, k), lambda j, i: (i, 0)),
                  pl.BlockSpec((tm, LANES), lambda j, i: (i, 0)), wspec, wspec],
        out_specs=pl.BlockSpec((tm, tf), lambda j, i: (i, j)),
        out_shape=jax.ShapeDtypeStruct((m, f), BF16),
        scratch_shapes=[pltpu.VMEM((k, tf), BF16), pltpu.VMEM((k, tf), BF16)],
        compiler_params=_params(2),
        name="ffn_up",
    )(u2, row_scale, w_gate, w_up)


def _ffn_down_kernel(a_ref, w_ref, h1_ref, g_ref, o_ref, acc, rs_sc, *, nk, k_last):
    s = pl.program_id(1)
    tk = a_ref.shape[1]
    n = acc.shape[0] * acc.shape[2]

    @pl.when(s == 0)
    def _():
        _accumulate_k_tile(a_ref[...], w_ref, acc, tk, first=True)

    @pl.when((s > 0) & (s < nk - 1))
    def _():
        _accumulate_k_tile(a_ref[...], w_ref, acc, tk)

    @pl.when(s == nk - 1)
    def _():
        _accumulate_k_tile(a_ref[...], w_ref, acc, k_last)

    @pl.when(s == nk)
    def _():
        rs_sc[...] = _row_rms_scale(acc, n)

    @pl.when(s >= nk)
    def _():
        y = acc[s - nk] * _lane_tile(rs_sc[...], acc.shape[2]) * g_ref[...]
        o_ref[...] = h1_ref[...] + y


def _ffn_down(act, w_down, h1, g, *, tm, tk, tc):
    m, f = act.shape
    n = w_down.shape[1]
    nk = pl.cdiv(f, tk)
    k_last = f - (nk - 1) * tk
    ne = n // tc
    assert m % tm == 0 and n % tc == 0 and k_last % LANES == 0
    kstep = lambda s: jnp.minimum(s, nk - 1)
    chunk = lambda s: jnp.clip(s - nk, 0, ne - 1)
    kern = functools.partial(_ffn_down_kernel, nk=nk, k_last=k_last)
    return pl.pallas_call(
        kern,
        grid=(m // tm, nk + ne),
        in_specs=[pl.BlockSpec((tm, tk), lambda i, s: (i, kstep(s))),
                  pl.BlockSpec((tk, n), lambda i, s: (kstep(s), 0)),
                  pl.BlockSpec((tm, tc), lambda i, s: (i, chunk(s))),
                  pl.BlockSpec((1, tc), lambda i, s: (0, chunk(s)))],
        out_specs=pl.BlockSpec((tm, tc), lambda i, s: (i, chunk(s))),
        out_shape=jax.ShapeDtypeStruct((m, n), F32),
        scratch_shapes=[pltpu.VMEM((ne, tm, tc), F32), pltpu.VMEM((tm, LANES), F32)],
        compiler_params=_params(2),
        name="ffn_down",
    )(act, w_down, h1, g.reshape(1, n))


def _rope_tables(n_pos):
    inv_freq = 1.0 / (ROPE_THETA ** (jnp.arange(0, DIFF_QK_DIM, 2, dtype=F32) / DIFF_QK_DIM))
    ang = jnp.arange(n_pos, dtype=F32)[:, None] * inv_freq[None, :]
    cos, sin = jnp.cos(ang), jnp.sin(ang)
    return jnp.concatenate([cos, cos], axis=-1), jnp.concatenate([-sin, sin], axis=-1)


class _Tiling(NamedTuple):
    rows: int
    cols: int
    ff_cols: int
    k_resident: int
    chunk_out_proj: int
    chunk_ffn_down: int
    epilogue_chunks: int
    norm_rows: int
    attn_q: int
    attn_kv: int
    attn_heads: int
    lru_rows: int
    lru_heads: int


def _tiling(seq, d_ff):
    t = _Tiling(rows=4 * MXU_DIM, cols=2 * MXU_DIM, ff_cols=MXU_DIM, k_resident=2 * MXU_DIM,
                chunk_out_proj=2 * MXU_DIM, chunk_ffn_down=4 * MXU_DIM, epilogue_chunks=4,
                norm_rows=MXU_DIM, attn_q=4 * MXU_DIM, attn_kv=2 * MXU_DIM, attn_heads=2,
                lru_rows=2 * MXU_DIM, lru_heads=4)
    assert seq % t.rows == 0 and seq % t.attn_q == 0 and seq % t.lru_rows == 0
    assert d_ff % t.ff_cols == 0
    return t


def kernel(x, meta_tokens, mix_pre_g, w_in, lambda_q1, lambda_k1, lambda_q2, lambda_k2, subln_g,
           conv_w, conv_b, w_r, b_r, w_i, b_i, lru_lambda, w_out, mix_post_g, ffn_pre_g, w_gate,
           w_up, w_down, ffn_post_g):
    batch, seq, d = x.shape
    n_meta = meta_tokens.shape[0]
    depth = w_in.shape[0]
    attn_w = d // 2
    lru_w = d - attn_w
    heads = attn_w // DIFF_V_DIM
    rows = batch * seq
    assert depth == 1, "the meta-prefix factorisation below is written for a single layer"
    assert n_meta == 2 * SUBLANES
    t = _tiling(seq, w_gate.shape[2])

    l = 0
    lambda_init = 0.8 - 0.6 * math.exp(-0.3 * l)
    scale = DIFF_QK_DIM ** -0.5 * math.log2(math.e)
    cos_all, sin_all = _rope_tables(n_meta + seq)
    cos_x, sin_x = cos_all[n_meta:], sin_all[n_meta:]
    cos_m, sin_m = cos_all[:n_meta], sin_all[:n_meta]

    x2d = x.reshape(rows, d)
    w_in_l = w_in[l]

    seq_tiles = seq // t.rows
    rope_x = functools.partial(_rope_epilogue, n_scaled=attn_w // t.cols, scale=scale)
    rope_m = functools.partial(_rope_epilogue, n_scaled=0, scale=scale)
    tab_x = pl.BlockSpec((t.rows, DIFF_QK_DIM), lambda j, i: (i % seq_tiles, 0))
    tab_m = pl.BlockSpec((n_meta, DIFF_QK_DIM), lambda j, i: (0, 0))

    u_x = _prenorm(x2d, mix_pre_g[l], t.norm_rows)
    u_m = _prenorm(meta_tokens.astype(x.dtype), mix_pre_g[l], n_meta)
    proj_x = functools.partial(_matmul_ws, u_x, w_in_l, tm=t.rows, tn=t.cols)
    proj_m = functools.partial(_matmul_ws, u_m, w_in_l, tm=n_meta, tn=t.cols)
    qk_x = proj_x(0, 2 * attn_w, out_dtype=BF16, epilogue=rope_x, extras=(cos_x, sin_x),
                  extra_specs=(tab_x, tab_x), row_chunks=t.epilogue_chunks, name="in_proj_qk")
    v_x = proj_x(2 * attn_w, attn_w, out_dtype=BF16, name="in_proj_v")
    xg_x = proj_x(3 * attn_w, 2 * lru_w, out_dtype=F32, name="in_proj_lru")
    k_m = proj_m(attn_w, attn_w, out_dtype=BF16, epilogue=rope_m, extras=(cos_m, sin_m),
                 extra_specs=(tab_m, tab_m), name="in_proj_k_meta")
    v_m = proj_m(2 * attn_w, attn_w, out_dtype=BF16, name="in_proj_v_meta")
    xg_m = proj_m(3 * attn_w, 2 * lru_w, out_dtype=F32, name="in_proj_lru_meta")

    attn = _diff_attention(qk_x, v_x, k_m, v_m, lambda_q1[l], lambda_k1[l], lambda_q2[l],
                           lambda_k2[l], subln_g[l], batch=batch, seq=seq, heads=heads,
                           lambda_init=lambda_init, tq=t.attn_q, tk=t.attn_kv, hps=t.attn_heads)

    lru_args = (conv_w[l], conv_b[l], w_r[l], b_r[l], w_i[l], b_i[l], lru_lambda[l])
    zeros8 = jnp.zeros((SUBLANES, lru_w), F32)
    _, h_meta = _rg_lru(xg_m, zeros8, zeros8, *lru_args, batch=1, seq=n_meta, tt=n_meta,
                        hc=t.lru_heads)
    tail_meta = xg_m[n_meta - SUBLANES:, :lru_w]
    rec, _ = _rg_lru(xg_x, tail_meta, h_meta, *lru_args, batch=batch, seq=seq, tt=t.lru_rows,
                     hc=t.lru_heads)

    h1, u2, rs2 = _out_proj(attn, rec, w_out[l], x2d, mix_post_g[l], ffn_pre_g[l], tm=t.rows,
                            tk=t.k_resident, tc=t.chunk_out_proj)

    act = _ffn_up(u2, rs2, w_gate[l], w_up[l], tm=t.rows, tf=t.ff_cols,
                  row_chunks=t.epilogue_chunks)
    out = _ffn_down(act, w_down[l], h1, ffn_post_g[l], tm=t.rows, tk=t.k_resident,
                    tc=t.chunk_ffn_down)
    return out.reshape(batch, seq, d)
```

```python
import functools
import math
from typing import NamedTuple

import jax
import jax.numpy as jnp
from jax import lax
from jax.experimental import pallas as pl
from jax.experimental.pallas import tpu as pltpu

DIFF_QK_DIM = 128
DIFF_V_DIM = 2 * DIFF_QK_DIM
LRU_HEADS = 16
CONV_WIDTH = 4
LRU_C = 8.0
ROPE_THETA = 10000.0
RMS_EPS = 1e-6

LANES = 128
SUBLANES = 8
MXU_DIM = 256
VMEM_BYTES = 64 * 1024 * 1024
VMEM_LIMIT_BYTES = VMEM_BYTES * 7 // 8

BF16 = jnp.bfloat16
F32 = jnp.float32


def _params(n_axes):
    return pltpu.CompilerParams(
        dimension_semantics=("arbitrary",) * n_axes,
        vmem_limit_bytes=VMEM_LIMIT_BYTES,
    )


def _rms_scale(x):
    return lax.rsqrt(jnp.mean(x * x, axis=-1, keepdims=True) + RMS_EPS)


def _cast_into(dst_ref, src_ref, chunk=256, rows=None):
    rows = src_ref.shape[0] if rows is None else rows
    chunk = min(chunk, rows)
    assert rows % chunk == 0

    def body(c, carry):
        r0 = pl.multiple_of(c * chunk, chunk)
        dst_ref[pl.ds(r0, chunk), :] = src_ref[pl.ds(r0, chunk), :].astype(dst_ref.dtype)
        return carry

    lax.fori_loop(0, rows // chunk, body, 0)


def _prenorm_kernel(x_ref, g_ref, o_ref):
    x = x_ref[...]
    o_ref[...] = (x * _rms_scale(x) * g_ref[...]).astype(o_ref.dtype)


def _prenorm(x2d, g, tm):
    rows, d = x2d.shape
    return pl.pallas_call(
        _prenorm_kernel,
        grid=(rows // tm,),
        in_specs=[pl.BlockSpec((tm, d), lambda i: (i, 0)),
                  pl.BlockSpec((1, d), lambda i: (0, 0))],
        out_specs=pl.BlockSpec((tm, d), lambda i: (i, 0)),
        out_shape=jax.ShapeDtypeStruct((rows, d), BF16),
        compiler_params=_params(1),
        name="prenorm",
    )(x2d, g.reshape(1, d))


def _mm_kernel(a_ref, w_ref, *rest, epilogue, n_extra, row_chunks):
    extra = rest[:n_extra]
    o_ref = rest[n_extra]
    wbf = rest[n_extra + 1]

    @pl.when(pl.program_id(1) == 0)
    def _():
        _cast_into(wbf, w_ref)

    chunk = a_ref.shape[0] // row_chunks
    for c in range(row_chunks):
        rows = slice(c * chunk, (c + 1) * chunk)
        acc = jnp.dot(a_ref[rows, :], wbf[...], preferred_element_type=F32)
        epilogue(acc, o_ref, rows, *extra)


def _plain_epilogue(acc, o_ref, rows):
    o_ref[rows, :] = acc.astype(o_ref.dtype)


def _rope_epilogue(acc, o_ref, rows, cos_ref, sin_ref, *, n_scaled, scale):
    j = pl.program_id(0)
    mult = jnp.where(j < n_scaled, jnp.float32(scale), jnp.float32(1.0))
    cos = cos_ref[rows, :] * mult
    sin = sin_ref[rows, :] * mult
    for c in range(acc.shape[1] // DIFF_QK_DIM):
        sl = slice(c * DIFF_QK_DIM, (c + 1) * DIFF_QK_DIM)
        y = acc[:, sl]
        o_ref[rows, sl] = (y * cos + pltpu.roll(y, DIFF_QK_DIM // 2, 1) * sin).astype(o_ref.dtype)


def _matmul_ws(a, w, col0, ncols, *, tm, tn, out_dtype, epilogue=_plain_epilogue,
               extras=(), extra_specs=(), row_chunks=1, name):
    m, k = a.shape
    assert w.shape[0] == k and m % tm == 0 and ncols % tn == 0 and col0 % tn == 0
    jb0 = col0 // tn
    kern = functools.partial(_mm_kernel, epilogue=epilogue, n_extra=len(extras),
                             row_chunks=row_chunks)
    return pl.pallas_call(
        kern,
        grid=(ncols // tn, m // tm),
        in_specs=[pl.BlockSpec((tm, k), lambda j, i: (i, 0)),
                  pl.BlockSpec((k, tn), lambda j, i: (0, jb0 + j)),
                  *extra_specs],
        out_specs=pl.BlockSpec((tm, tn), lambda j, i: (i, j)),
        out_shape=jax.ShapeDtypeStruct((m, ncols), out_dtype),
        scratch_shapes=[pltpu.VMEM((k, tn), BF16)],
        compiler_params=_params(2),
        name=name,
    )(a, w, *extras)


def _dot_nt(a, b):
    return lax.dot_general(a, b, (((1,), (1,)), ((), ())), preferred_element_type=F32)


def _attn_kernel(lq1_ref, lk1_ref, lq2_ref, lk2_ref, g_ref, q_ref, k_ref, v_ref, km_ref, vm_ref,
                 o_ref, acc_sc, m_sc, l_sc, *, tq, tk, n_meta, lambda_init):
    assert tq % tk == 0
    qi = pl.program_id(2)
    dk, dv = DIFF_QK_DIM, DIFF_V_DIM
    n_meta_pad = km_ref.shape[0]
    streams = [(2 * hh + c, slice(hh * dv + c * dk, hh * dv + (c + 1) * dk),
                slice(hh * dv, (hh + 1) * dv))
               for hh in range(q_ref.shape[1] // dv) for c in range(2)]

    def update(c, rows, blocks, first):
        nrows = blocks[0][0].shape[0]
        groups = [s[:, g * LANES:(g + 1) * LANES] for s, _ in blocks
                  for g in range(s.shape[1] // LANES)]
        mx = groups[0]
        for grp in groups[1:]:
            mx = jnp.maximum(mx, grp)
        m_cur = jnp.max(mx, axis=1, keepdims=True)
        if first:
            m_new = jnp.broadcast_to(m_cur, (nrows, LANES))
        else:
            m_prev = m_sc[c, rows]
            m_new = jnp.maximum(m_prev, m_cur)
            alpha = jnp.exp2(m_prev - m_new)
        ps = [jnp.exp2(grp - m_new) for grp in groups]
        lsum = ps[0]
        for p in ps[1:]:
            lsum = lsum + p
        pv = None
        g0 = 0
        for s, v in blocks:
            ng = s.shape[1] // LANES
            p = jnp.concatenate([p.astype(BF16) for p in ps[g0:g0 + ng]], axis=1)
            g0 += ng
            d = jnp.dot(p, v, preferred_element_type=F32)
            pv = d if pv is None else pv + d
        if first:
            l_sc[c, rows] = lsum
            acc_sc[c, rows] = pv
        else:
            l_sc[c, rows] = alpha * l_sc[c, rows] + lsum
            acc_sc[c, rows] = (jnp.concatenate([alpha] * (DIFF_V_DIM // LANES), axis=1)
                               * acc_sc[c, rows] + pv)
        m_sc[c, rows] = m_new

    nsub = tq // tk
    meta_ok = lax.broadcasted_iota(jnp.int32, (tk, n_meta_pad), 1) < n_meta
    causal = (lax.broadcasted_iota(jnp.int32, (tk, tk), 1)
              <= lax.broadcasted_iota(jnp.int32, (tk, tk), 0))
    for r in range(nsub):
        rows = slice(r * tk, (r + 1) * tk)
        for c, cs, vs in streams:
            q = q_ref[rows, cs]
            blocks = [(jnp.where(meta_ok, _dot_nt(q, km_ref[:, cs]), -jnp.inf), vm_ref[:, vs])]
            for j in range(r + 1):
                ks = pl.multiple_of((qi * nsub + j) * tk, tk)
                s = _dot_nt(q, k_ref[pl.ds(ks, tk), cs])
                if j == r:
                    s = jnp.where(causal, s, -jnp.inf)
                blocks.append((s, v_ref[pl.ds(ks, tk), vs]))
            update(c, rows, blocks, True)

    def tile(ks):
        for c, cs, vs in streams:
            s = _dot_nt(q_ref[:, cs], k_ref[pl.ds(ks, tk), cs])
            update(c, slice(0, tq), [(s, v_ref[pl.ds(ks, tk), vs])], False)

    n_full = qi * nsub

    def body(pair, carry):
        tile(pl.multiple_of(2 * pair * tk, tk))
        tile(pl.multiple_of((2 * pair + 1) * tk, tk))
        return carry

    lax.fori_loop(0, n_full // 2, body, 0)

    if nsub % 2 == 1:
        @pl.when(n_full % 2 == 1)
        def _():
            tile(pl.multiple_of((n_full - 1) * tk, tk))

    lam = (jnp.exp(jnp.sum(lq1_ref[...] * lk1_ref[...], axis=1, keepdims=True))
           - jnp.exp(jnp.sum(lq2_ref[...] * lk2_ref[...], axis=1, keepdims=True))
           + lambda_init)
    for c0, _, vs in streams[::2]:
        l0 = jnp.sum(l_sc[c0], axis=1, keepdims=True)
        l1 = jnp.sum(l_sc[c0 + 1], axis=1, keepdims=True)
        o = acc_sc[c0] / l0 - lam * (acc_sc[c0 + 1] / l1)
        o = o * _rms_scale(o) * g_ref[...] * (1.0 - lambda_init)
        o_ref[:, vs] = o.astype(o_ref.dtype)


def _diff_attention(qk, v, k_meta, v_meta, lq1, lk1, lq2, lk2, subln_g, *, batch, seq, heads,
                    lambda_init, tq, tk, hps):
    assert heads % hps == 0
    dv = DIFF_V_DIM
    bw = hps * dv
    hgroups = heads // hps
    nq = seq // tq
    n_meta = k_meta.shape[0]
    n_meta_pad = LANES
    assert n_meta <= n_meta_pad
    k_meta = jnp.pad(k_meta, ((0, n_meta_pad - n_meta), (0, 0)))
    v_meta = jnp.pad(v_meta, ((0, n_meta_pad - n_meta), (0, 0)))
    vec = lambda a: a.reshape(1, -1).astype(F32)
    small = lambda n: pl.BlockSpec((1, n), lambda b, h, i: (0, 0))
    kern = functools.partial(_attn_kernel, tq=tq, tk=tk, n_meta=n_meta, lambda_init=lambda_init)
    return pl.pallas_call(
        kern,
        grid=(batch, hgroups, nq),
        in_specs=[small(DIFF_QK_DIM)] * 4 + [
            small(dv),
            pl.BlockSpec((tq, bw), lambda b, h, i: (b * nq + i, h)),
            pl.BlockSpec((seq, bw), lambda b, h, i: (b, hgroups + h)),
            pl.BlockSpec((seq, bw), lambda b, h, i: (b, h)),
            pl.BlockSpec((n_meta_pad, bw), lambda b, h, i: (0, h)),
            pl.BlockSpec((n_meta_pad, bw), lambda b, h, i: (0, h)),
        ],
        out_specs=pl.BlockSpec((tq, bw), lambda b, h, i: (b * nq + i, h)),
        out_shape=jax.ShapeDtypeStruct((batch * seq, heads * dv), BF16),
        scratch_shapes=[pltpu.VMEM((2 * hps, tq, dv), F32),
                        pltpu.VMEM((2 * hps, tq, LANES), F32),
                        pltpu.VMEM((2 * hps, tq, LANES), F32)],
        compiler_params=_params(3),
        name="diff_attention",
    )(vec(lq1), vec(lk1), vec(lq2), vec(lk2), vec(subln_g), qk, qk, v, k_meta, v_meta)


def _gelu_tanh(x):
    c = math.sqrt(2.0 / math.pi)
    return 0.5 * x * (1.0 + jnp.tanh(c * (x + 0.044715 * (x * x * x))))


def _sigmoid(x):
    return 0.5 * jnp.tanh(0.5 * x) + 0.5


def _lru_kernel(xr_ref, gate_ref, tail0_ref, h0_ref, cw_ref, cb_ref, wr_ref, br_ref, wi_ref,
                bi_ref, lam_ref, o_ref, hlast_ref, xs_sc, h_sc, *, tt, hc):
    t = pl.program_id(2)
    blk = LANES
    halo = SUBLANES

    @pl.when(t == 0)
    def _():
        xs_sc[0:halo, :] = tail0_ref[...]
        h_sc[...] = h0_ref[...]

    xs_sc[halo:halo + tt, :] = xr_ref[...]
    xc = cb_ref[...]
    for w in range(CONV_WIDTH):
        start = halo - (CONV_WIDTH - 1) + w
        xc = xc + cw_ref[w:w + 1, :] * xs_sc[pl.ds(start, tt), :]
    xs_sc[0:halo, :] = xr_ref[tt - halo:tt, :]

    r_parts, i_parts = [], []
    for hh in range(hc):
        xh = xc[:, hh * blk:(hh + 1) * blk].astype(BF16)
        r_parts.append(jnp.dot(xh, wr_ref[hh].astype(BF16), preferred_element_type=F32))
        i_parts.append(jnp.dot(xh, wi_ref[hh].astype(BF16), preferred_element_type=F32))
    r = _sigmoid(jnp.concatenate(r_parts, axis=1) + br_ref[...])
    ig = _sigmoid(jnp.concatenate(i_parts, axis=1) + bi_ref[...])

    neg_lam = -lam_ref[...]
    softplus = jnp.maximum(neg_lam, 0.0) + jnp.log1p(jnp.exp(-jnp.abs(neg_lam)))
    log_a = (-LRU_C) * r * softplus
    a = jnp.exp(log_a)
    mult = jnp.sqrt(-jnp.tanh(log_a) * (a * a + 1.0))
    b = mult * (ig * xc)

    ngroups = tt // SUBLANES
    a = a.reshape(ngroups, SUBLANES, a.shape[1])
    b = b.reshape(a.shape)
    row = lax.broadcasted_iota(jnp.int32, (1,) + a.shape[1:], 1)
    d = 1
    while d < SUBLANES:
        keep = row >= d
        a_sh = jnp.where(keep, pltpu.roll(a, d, 1), 1.0)
        b_sh = jnp.where(keep, pltpu.roll(b, d, 1), 0.0)
        b = a * b_sh + b
        a = a * a_sh
        d *= 2
    carry = h_sc[...]
    groups = []
    for j in range(ngroups):
        hj = a[j] * carry + b[j]
        groups.append(hj)
        carry = jnp.broadcast_to(hj[SUBLANES - 1:SUBLANES], hj.shape)
    h = jnp.concatenate(groups, axis=0)
    h_sc[...] = carry
    hlast_ref[...] = carry
    o_ref[...] = (h * _gelu_tanh(gate_ref[...])).astype(o_ref.dtype)


def _rg_lru(xg, tail0, h0, conv_w, conv_b, w_r, b_r, w_i, b_i, lru_lambda, *, batch, seq, tt, hc):
    width = conv_w.shape[1]
    cw = hc * LANES
    ngrp = width // cw
    nt = seq // tt
    row = lambda a: a.reshape(1, width).astype(F32)
    chan = lambda r: pl.BlockSpec((r, cw), lambda b, g, t: (0, g))
    wspec = pl.BlockSpec((hc, LANES, LANES), lambda b, g, t: (g, 0, 0))
    kern = functools.partial(_lru_kernel, tt=tt, hc=hc)
    return pl.pallas_call(
        kern,
        grid=(batch, ngrp, nt),
        in_specs=[
            pl.BlockSpec((tt, cw), lambda b, g, t: (b * nt + t, g)),
            pl.BlockSpec((tt, cw), lambda b, g, t: (b * nt + t, ngrp + g)),
            chan(SUBLANES), chan(SUBLANES),
            chan(CONV_WIDTH), chan(1),
            wspec, chan(1), wspec, chan(1), chan(1),
        ],
        out_specs=[pl.BlockSpec((tt, cw), lambda b, g, t: (b * nt + t, g)),
                   pl.BlockSpec((SUBLANES, cw), lambda b, g, t: (b, g))],
        out_shape=[jax.ShapeDtypeStruct((batch * seq, width), BF16),
                   jax.ShapeDtypeStruct((batch * SUBLANES, width), F32)],
        scratch_shapes=[pltpu.VMEM((tt + SUBLANES, cw), F32),
                        pltpu.VMEM((SUBLANES, cw), F32)],
        compiler_params=_params(3),
        name="rg_lru",
    )(xg, xg, tail0, h0, conv_w.astype(F32), row(conv_b), w_r, row(b_r), w_i, row(b_i),
      row(lru_lambda))


def _accumulate_k_tile(a, w_ref, acc, rows, first=False):
    a = a[:, :rows]
    tc = acc.shape[2]
    for e in range(acc.shape[0]):
        w = w_ref[0:rows, e * tc:(e + 1) * tc].astype(BF16)
        d = jnp.dot(a, w, preferred_element_type=F32)
        acc[e] = d if first else acc[e] + d


def _row_rms_scale(acc, n):
    ssq = jnp.sum(acc[0] * acc[0], axis=1, keepdims=True)
    for e in range(1, acc.shape[0]):
        ssq = ssq + jnp.sum(acc[e] * acc[e], axis=1, keepdims=True)
    return jnp.broadcast_to(lax.rsqrt(ssq / n + RMS_EPS), (acc.shape[1], LANES))


def _lane_tile(x, width):
    return jnp.concatenate([x] * (width // x.shape[1]), axis=1)


def _out_proj_kernel(attn_ref, rec_ref, w_ref, x_ref, g1_ref, g2_ref, h1_ref, u2_ref, rs2_ref, acc,
                     rs_sc, ssq_sc, *, nk, ne):
    s = pl.program_id(1)
    tk = attn_ref.shape[1]
    tc = acc.shape[2]
    n = acc.shape[0] * tc

    @pl.when(s == 0)
    def _():
        _accumulate_k_tile(attn_ref[...], w_ref, acc, tk, first=True)

    @pl.when((s > 0) & (s < nk // 2))
    def _():
        _accumulate_k_tile(attn_ref[...], w_ref, acc, tk)

    @pl.when((s >= nk // 2) & (s < nk))
    def _():
        _accumulate_k_tile(rec_ref[...], w_ref, acc, tk)

    @pl.when(s == nk)
    def _():
        rs_sc[...] = _row_rms_scale(acc, n)
        ssq_sc[...] = jnp.zeros_like(ssq_sc)

    @pl.when(s >= nk)
    def _():
        h1 = x_ref[...] + acc[s - nk] * _lane_tile(rs_sc[...], tc) * g1_ref[...]
        h1_ref[...] = h1
        u2_ref[...] = (h1 * g2_ref[...]).astype(u2_ref.dtype)
        ssq_sc[...] += jnp.broadcast_to(jnp.sum(h1 * h1, axis=1, keepdims=True), ssq_sc.shape)

    @pl.when(s == nk + ne - 1)
    def _():
        rs2_ref[...] = lax.rsqrt(ssq_sc[...] / n + RMS_EPS)


def _out_proj(attn, rec, w_out, x2d, g_post, g_pre, *, tm, tk, tc):
    m, k_half = attn.shape
    n = w_out.shape[1]
    assert rec.shape == attn.shape and w_out.shape[0] == 2 * k_half
    assert m % tm == 0 and k_half % tk == 0 and n % tc == 0
    nkh = k_half // tk
    nk = 2 * nkh
    ne = n // tc
    chunk = lambda s: jnp.clip(s - nk, 0, ne - 1)
    kern = functools.partial(_out_proj_kernel, nk=nk, ne=ne)
    return pl.pallas_call(
        kern,
        grid=(m // tm, nk + ne),
        in_specs=[pl.BlockSpec((tm, tk), lambda i, s: (i, jnp.minimum(s, nkh - 1))),
                  pl.BlockSpec((tm, tk), lambda i, s: (i, jnp.clip(s - nkh, 0, nkh - 1))),
                  pl.BlockSpec((tk, n), lambda i, s: (jnp.minimum(s, nk - 1), 0)),
                  pl.BlockSpec((tm, tc), lambda i, s: (i, chunk(s))),
                  pl.BlockSpec((1, tc), lambda i, s: (0, chunk(s))),
                  pl.BlockSpec((1, tc), lambda i, s: (0, chunk(s)))],
        out_specs=[pl.BlockSpec((tm, tc), lambda i, s: (i, chunk(s))),
                   pl.BlockSpec((tm, tc), lambda i, s: (i, chunk(s))),
                   pl.BlockSpec((tm, LANES), lambda i, s: (i, 0))],
        out_shape=[jax.ShapeDtypeStruct((m, n), F32), jax.ShapeDtypeStruct((m, n), BF16),
                   jax.ShapeDtypeStruct((m, LANES), F32)],
        scratch_shapes=[pltpu.VMEM((ne, tm, tc), F32), pltpu.VMEM((tm, LANES), F32),
                        pltpu.VMEM((tm, LANES), F32)],
        compiler_params=_params(2),
        name="out_proj",
    )(attn, rec, w_out, x2d, g_post.reshape(1, n), g_pre.reshape(1, n))


def _ffn_up_kernel(a_ref, rs_ref, wg_ref, wu_ref, o_ref, *, nj, last_cols):
    rs = _lane_tile(rs_ref[...], MXU_DIM)

    def emit(ncols):
        for c in range(ncols // MXU_DIM):
            cols = slice(c * MXU_DIM, (c + 1) * MXU_DIM)
            g = jnp.dot(a_ref[...], wg_ref[:, cols].astype(BF16), preferred_element_type=F32) * rs
            u = jnp.dot(a_ref[...], wu_ref[:, cols].astype(BF16), preferred_element_type=F32) * rs
            o_ref[:, cols] = (g * jax.nn.sigmoid(g) * u).astype(o_ref.dtype)

    tf = o_ref.shape[1]
    if last_cols == tf:
        emit(tf)
    else:
        j = pl.program_id(1)
        pl.when(j < nj - 1)(lambda: emit(tf))
        pl.when(j == nj - 1)(lambda: emit(last_cols))


def _ffn_up(u2, row_scale, w_gate, w_up, *, tm, tf):
    m, k = u2.shape
    f = w_gate.shape[1]
    nj = pl.cdiv(f, tf)
    last_cols = f - (nj - 1) * tf
    assert m % tm == 0 and tf % MXU_DIM == 0 and last_cols % MXU_DIM == 0
    wspec = pl.BlockSpec((k, tf), lambda i, j: (0, j))
    return pl.pallas_call(
        functools.partial(_ffn_up_kernel, nj=nj, last_cols=last_cols),
        grid=(m // tm, nj),
        in_specs=[pl.BlockSpec((tm, k), lambda i, j: (i, 0)),
                  pl.BlockSpec((tm, LANES), lambda i, j: (i, 0)), wspec, wspec],
        out_specs=pl.BlockSpec((tm, tf), lambda i, j: (i, j)),
        out_shape=jax.ShapeDtypeStruct((m, f), BF16),
        compiler_params=_params(2),
        name="ffn_up",
    )(u2, row_scale, w_gate, w_up)


def _ffn_down_kernel(a_ref, w_ref, h1_ref, g_ref, o_ref, acc, rs_sc, *, nk, k_last):
    s = pl.program_id(1)
    tk = a_ref.shape[1]
    n = acc.shape[0] * acc.shape[2]

    @pl.when(s == 0)
    def _():
        _accumulate_k_tile(a_ref[...], w_ref, acc, tk, first=True)

    @pl.when((s > 0) & (s < nk - 1))
    def _():
        _accumulate_k_tile(a_ref[...], w_ref, acc, tk)

    @pl.when(s == nk - 1)
    def _():
        _accumulate_k_tile(a_ref[...], w_ref, acc, k_last)

    @pl.when(s == nk)
    def _():
        rs_sc[...] = _row_rms_scale(acc, n)

    @pl.when(s >= nk)
    def _():
        y = acc[s - nk] * _lane_tile(rs_sc[...], acc.shape[2]) * g_ref[...]
        o_ref[...] = h1_ref[...] + y


def _ffn_down(act, w_down, h1, g, *, tm, tk, tc):
    m, f = act.shape
    n = w_down.shape[1]
    nk = pl.cdiv(f, tk)
    k_last = f - (nk - 1) * tk
    ne = n // tc
    assert m % tm == 0 and n % tc == 0 and k_last % LANES == 0
    kstep = lambda s: jnp.minimum(s, nk - 1)
    chunk = lambda s: jnp.clip(s - nk, 0, ne - 1)
    kern = functools.partial(_ffn_down_kernel, nk=nk, k_last=k_last)
    return pl.pallas_call(
        kern,
        grid=(m // tm, nk + ne),
        in_specs=[pl.BlockSpec((tm, tk), lambda i, s: (i, kstep(s))),
                  pl.BlockSpec((tk, n), lambda i, s: (kstep(s), 0)),
                  pl.BlockSpec((tm, tc), lambda i, s: (i, chunk(s))),
                  pl.BlockSpec((1, tc), lambda i, s: (0, chunk(s)))],
        out_specs=pl.BlockSpec((tm, tc), lambda i, s: (i, chunk(s))),
        out_shape=jax.ShapeDtypeStruct((m, n), F32),
        scratch_shapes=[pltpu.VMEM((ne, tm, tc), F32), pltpu.VMEM((tm, LANES), F32)],
        compiler_params=_params(2),
        name="ffn_down",
    )(act, w_down, h1, g.reshape(1, n))


def _rope_tables(n_pos):
    inv_freq = 1.0 / (ROPE_THETA ** (jnp.arange(0, DIFF_QK_DIM, 2, dtype=F32) / DIFF_QK_DIM))
    ang = jnp.arange(n_pos, dtype=F32)[:, None] * inv_freq[None, :]
    cos, sin = jnp.cos(ang), jnp.sin(ang)
    return jnp.concatenate([cos, cos], axis=-1), jnp.concatenate([-sin, sin], axis=-1)


class _Tiling(NamedTuple):
    rows: int
    cols: int
    ff_cols: int
    k_resident: int
    chunk_out_proj: int
    chunk_ffn_down: int
    epilogue_chunks: int
    norm_rows: int
    attn_q: int
    attn_kv: int
    attn_heads: int
    lru_rows: int
    lru_heads: int


def _tiling(seq, d_ff):
    t = _Tiling(rows=4 * MXU_DIM, cols=2 * MXU_DIM, ff_cols=2 * MXU_DIM, k_resident=2 * MXU_DIM,
                chunk_out_proj=2 * MXU_DIM, chunk_ffn_down=4 * MXU_DIM, epilogue_chunks=4,
                norm_rows=MXU_DIM, attn_q=4 * MXU_DIM, attn_kv=2 * MXU_DIM, attn_heads=2,
                lru_rows=4 * MXU_DIM, lru_heads=4)
    assert seq % t.rows == 0 and seq % t.attn_q == 0 and seq % t.lru_rows == 0
    assert d_ff % MXU_DIM == 0
    return t


def kernel(x, meta_tokens, mix_pre_g, w_in, lambda_q1, lambda_k1, lambda_q2, lambda_k2, subln_g,
           conv_w, conv_b, w_r, b_r, w_i, b_i, lru_lambda, w_out, mix_post_g, ffn_pre_g, w_gate,
           w_up, w_down, ffn_post_g):
    batch, seq, d = x.shape
    n_meta = meta_tokens.shape[0]
    depth = w_in.shape[0]
    attn_w = d // 2
    lru_w = d - attn_w
    heads = attn_w // DIFF_V_DIM
    rows = batch * seq
    assert depth == 1, "the meta-prefix factorisation below is written for a single layer"
    assert n_meta == 2 * SUBLANES
    t = _tiling(seq, w_gate.shape[2])

    l = 0
    lambda_init = 0.8 - 0.6 * math.exp(-0.3 * l)
    scale = DIFF_QK_DIM ** -0.5 * math.log2(math.e)
    cos_all, sin_all = _rope_tables(n_meta + seq)
    cos_x, sin_x = cos_all[n_meta:], sin_all[n_meta:]
    cos_m, sin_m = cos_all[:n_meta], sin_all[:n_meta]

    x2d = x.reshape(rows, d)
    w_in_l = w_in[l]

    seq_tiles = seq // t.rows
    rope_x = functools.partial(_rope_epilogue, n_scaled=attn_w // t.cols, scale=scale)
    rope_m = functools.partial(_rope_epilogue, n_scaled=0, scale=scale)
    tab_x = pl.BlockSpec((t.rows, DIFF_QK_DIM), lambda j, i: (i % seq_tiles, 0))
    tab_m = pl.BlockSpec((n_meta, DIFF_QK_DIM), lambda j, i: (0, 0))

    u_x = _prenorm(x2d, mix_pre_g[l], t.norm_rows)
    u_m = _prenorm(meta_tokens.astype(x.dtype), mix_pre_g[l], n_meta)
    proj_x = functools.partial(_matmul_ws, u_x, w_in_l, tm=t.rows, tn=t.cols)
    proj_m = functools.partial(_matmul_ws, u_m, w_in_l, tm=n_meta, tn=t.cols)
    qk_x = proj_x(0, 2 * attn_w, out_dtype=BF16, epilogue=rope_x, extras=(cos_x, sin_x),
                  extra_specs=(tab_x, tab_x), row_chunks=t.epilogue_chunks, name="in_proj_qk")
    v_x = proj_x(2 * attn_w, attn_w, out_dtype=BF16, name="in_proj_v")
    xg_x = proj_x(3 * attn_w, 2 * lru_w, out_dtype=F32, name="in_proj_lru")
    k_m = proj_m(attn_w, attn_w, out_dtype=BF16, epilogue=rope_m, extras=(cos_m, sin_m),
                 extra_specs=(tab_m, tab_m), name="in_proj_k_meta")
    v_m = proj_m(2 * attn_w, attn_w, out_dtype=BF16, name="in_proj_v_meta")
    xg_m = proj_m(3 * attn_w, 2 * lru_w, out_dtype=F32, name="in_proj_lru_meta")

    attn = _diff_attention(qk_x, v_x, k_m, v_m, lambda_q1[l], lambda_k1[l], lambda_q2[l],
                           lambda_k2[l], subln_g[l], batch=batch, seq=seq, heads=heads,
                           lambda_init=lambda_init, tq=t.attn_q, tk=t.attn_kv, hps=t.attn_heads)

    lru_args = (conv_w[l], conv_b[l], w_r[l], b_r[l], w_i[l], b_i[l], lru_lambda[l])
    zeros8 = jnp.zeros((SUBLANES, lru_w), F32)
    _, h_meta = _rg_lru(xg_m, zeros8, zeros8, *lru_args, batch=1, seq=n_meta, tt=n_meta,
                        hc=t.lru_heads)
    tail_meta = xg_m[n_meta - SUBLANES:, :lru_w]
    rec, _ = _rg_lru(xg_x, tail_meta, h_meta, *lru_args, batch=batch, seq=seq, tt=t.lru_rows,
                     hc=t.lru_heads)

    h1, u2, rs2 = _out_proj(attn, rec, w_out[l], x2d, mix_post_g[l], ffn_pre_g[l], tm=t.rows,
                            tk=t.k_resident, tc=t.chunk_out_proj)

    act = _ffn_up(u2, rs2, w_gate[l], w_up[l], tm=t.rows, tf=t.ff_cols)
    out = _ffn_down(act, w_down[l], h1, ffn_post_g[l], tm=t.rows, tk=t.k_resident,
                    tc=t.chunk_ffn_down)
    return out.reshape(batch, seq, d)
```

```python
import functools
import math
from typing import NamedTuple

import jax
import jax.numpy as jnp
from jax import lax
from jax.experimental import pallas as pl
from jax.experimental.pallas import tpu as pltpu

DIFF_QK_DIM = 128
DIFF_V_DIM = 2 * DIFF_QK_DIM
LRU_HEADS = 16
CONV_WIDTH = 4
LRU_C = 8.0
ROPE_THETA = 10000.0
RMS_EPS = 1e-6

LANES = 128
SUBLANES = 8
MXU_DIM = 256
VMEM_BYTES = 64 * 1024 * 1024
VMEM_LIMIT_BYTES = VMEM_BYTES * 7 // 8

BF16 = jnp.bfloat16
F32 = jnp.float32


def _params(n_axes):
    return pltpu.CompilerParams(
        dimension_semantics=("arbitrary",) * n_axes,
        vmem_limit_bytes=VMEM_LIMIT_BYTES,
    )


def _rms_scale(x):
    return lax.rsqrt(jnp.mean(x * x, axis=-1, keepdims=True) + RMS_EPS)


def _prenorm_kernel(x_ref, g_ref, o_ref):
    x = x_ref[...]
    o_ref[...] = (x * _rms_scale(x) * g_ref[...]).astype(o_ref.dtype)


def _prenorm(x2d, g, tm):
    rows, d = x2d.shape
    return pl.pallas_call(
        _prenorm_kernel,
        grid=(rows // tm,),
        in_specs=[pl.BlockSpec((tm, d), lambda i: (i, 0)),
                  pl.BlockSpec((1, d), lambda i: (0, 0))],
        out_specs=pl.BlockSpec((tm, d), lambda i: (i, 0)),
        out_shape=jax.ShapeDtypeStruct((rows, d), BF16),
        compiler_params=_params(1),
        name="prenorm",
    )(x2d, g.reshape(1, d))


def _mm_kernel(a_ref, w_ref, *rest, prepare, store, n_extra):
    extra = rest[:n_extra]
    o_ref = rest[n_extra]
    ctx = prepare(*extra)
    for c in range(o_ref.shape[1] // MXU_DIM):
        cols = slice(c * MXU_DIM, (c + 1) * MXU_DIM)
        acc = jnp.dot(a_ref[...], w_ref[:, cols].astype(BF16), preferred_element_type=F32)
        store(acc, o_ref, cols, ctx)


def _no_prepare():
    return None


def _plain_store(acc, o_ref, cols, ctx):
    o_ref[:, cols] = acc.astype(o_ref.dtype)


def _rope_prepare(cos_ref, sin_ref, *, n_scaled, scale):
    mult = jnp.where(pl.program_id(1) < n_scaled, jnp.float32(scale), jnp.float32(1.0))
    return cos_ref[...] * mult, sin_ref[...] * mult


def _rope_store(acc, o_ref, cols, tables):
    cos, sin = tables
    for h in range(acc.shape[1] // DIFF_QK_DIM):
        y = acc[:, h * DIFF_QK_DIM:(h + 1) * DIFF_QK_DIM]
        y = y * cos + pltpu.roll(y, DIFF_QK_DIM // 2, 1) * sin
        c0 = cols.start + h * DIFF_QK_DIM
        o_ref[:, c0:c0 + DIFF_QK_DIM] = y.astype(o_ref.dtype)


def _matmul_os(a, w, col0, ncols, *, tm, tn, out_dtype, prepare=_no_prepare, store=_plain_store,
               extras=(), extra_specs=(), name):
    m, k = a.shape
    assert w.shape[0] == k and m % tm == 0 and ncols % tn == 0 and col0 % tn == 0
    assert tn % MXU_DIM == 0
    jb0 = col0 // tn
    kern = functools.partial(_mm_kernel, prepare=prepare, store=store, n_extra=len(extras))
    return pl.pallas_call(
        kern,
        grid=(m // tm, ncols // tn),
        in_specs=[pl.BlockSpec((tm, k), lambda i, j: (i, 0)),
                  pl.BlockSpec((k, tn), lambda i, j: (0, jb0 + j)),
                  *extra_specs],
        out_specs=pl.BlockSpec((tm, tn), lambda i, j: (i, j)),
        out_shape=jax.ShapeDtypeStruct((m, ncols), out_dtype),
        compiler_params=_params(2),
        name=name,
    )(a, w, *extras)


def _dot_nt(a, b):
    return lax.dot_general(a, b, (((1,), (1,)), ((), ())), preferred_element_type=F32)


def _attn_kernel(lq1_ref, lk1_ref, lq2_ref, lk2_ref, g_ref, q_ref, k_ref, v_ref, km_ref, vm_ref,
                 o_ref, acc_sc, m_sc, l_sc, *, tq, tk, n_meta, lambda_init):
    assert tq % tk == 0
    qi = pl.program_id(2)
    dk, dv = DIFF_QK_DIM, DIFF_V_DIM
    n_meta_pad = km_ref.shape[0]
    streams = [(2 * hh + c, slice(hh * dv + c * dk, hh * dv + (c + 1) * dk),
                slice(hh * dv, (hh + 1) * dv))
               for hh in range(q_ref.shape[1] // dv) for c in range(2)]

    def update(c, rows, blocks, first):
        nrows = blocks[0][0].shape[0]
        groups = [s[:, g * LANES:(g + 1) * LANES] for s, _ in blocks
                  for g in range(s.shape[1] // LANES)]
        mx = groups[0]
        for grp in groups[1:]:
            mx = jnp.maximum(mx, grp)
        m_cur = jnp.max(mx, axis=1, keepdims=True)
        if first:
            m_new = jnp.broadcast_to(m_cur, (nrows, LANES))
        else:
            m_prev = m_sc[c, rows]
            m_new = jnp.maximum(m_prev, m_cur)
            alpha = jnp.exp2(m_prev - m_new)
        ps = [jnp.exp2(grp - m_new) for grp in groups]
        lsum = ps[0]
        for p in ps[1:]:
            lsum = lsum + p
        pv = None
        g0 = 0
        for s, v in blocks:
            ng = s.shape[1] // LANES
            p = jnp.concatenate([p.astype(BF16) for p in ps[g0:g0 + ng]], axis=1)
            g0 += ng
            d = jnp.dot(p, v, preferred_element_type=F32)
            pv = d if pv is None else pv + d
        if first:
            l_sc[c, rows] = lsum
            acc_sc[c, rows] = pv
        else:
            l_sc[c, rows] = alpha * l_sc[c, rows] + lsum
            acc_sc[c, rows] = (jnp.concatenate([alpha] * (DIFF_V_DIM // LANES), axis=1)
                               * acc_sc[c, rows] + pv)
        m_sc[c, rows] = m_new

    nsub = tq // tk
    meta_ok = lax.broadcasted_iota(jnp.int32, (tk, n_meta_pad), 1) < n_meta
    causal = (lax.broadcasted_iota(jnp.int32, (tk, tk), 1)
              <= lax.broadcasted_iota(jnp.int32, (tk, tk), 0))
    for r in range(nsub):
        rows = slice(r * tk, (r + 1) * tk)
        for c, cs, vs in streams:
            q = q_ref[rows, cs]
            blocks = [(jnp.where(meta_ok, _dot_nt(q, km_ref[:, cs]), -jnp.inf), vm_ref[:, vs])]
            for j in range(r + 1):
                ks = pl.multiple_of((qi * nsub + j) * tk, tk)
                s = _dot_nt(q, k_ref[pl.ds(ks, tk), cs])
                if j == r:
                    s = jnp.where(causal, s, -jnp.inf)
                blocks.append((s, v_ref[pl.ds(ks, tk), vs]))
            update(c, rows, blocks, True)

    def tile(ks):
        for c, cs, vs in streams:
            s = _dot_nt(q_ref[:, cs], k_ref[pl.ds(ks, tk), cs])
            update(c, slice(0, tq), [(s, v_ref[pl.ds(ks, tk), vs])], False)

    n_full = qi * nsub

    def body(pair, carry):
        tile(pl.multiple_of(2 * pair * tk, tk))
        tile(pl.multiple_of((2 * pair + 1) * tk, tk))
        return carry

    lax.fori_loop(0, n_full // 2, body, 0)

    if nsub % 2 == 1:
        @pl.when(n_full % 2 == 1)
        def _():
            tile(pl.multiple_of((n_full - 1) * tk, tk))

    lam = (jnp.exp(jnp.sum(lq1_ref[...] * lk1_ref[...], axis=1, keepdims=True))
           - jnp.exp(jnp.sum(lq2_ref[...] * lk2_ref[...], axis=1, keepdims=True))
           + lambda_init)
    for c0, _, vs in streams[::2]:
        l0 = jnp.sum(l_sc[c0], axis=1, keepdims=True)
        l1 = jnp.sum(l_sc[c0 + 1], axis=1, keepdims=True)
        o = acc_sc[c0] / l0 - lam * (acc_sc[c0 + 1] / l1)
        o = o * _rms_scale(o) * g_ref[...] * (1.0 - lambda_init)
        o_ref[:, vs] = o.astype(o_ref.dtype)


def _diff_attention(qk, v, k_meta, v_meta, lq1, lk1, lq2, lk2, subln_g, *, batch, seq, heads,
                    lambda_init, tq, tk, hps):
    assert heads % hps == 0
    dv = DIFF_V_DIM
    bw = hps * dv
    hgroups = heads // hps
    nq = seq // tq
    n_meta = k_meta.shape[0]
    n_meta_pad = LANES
    assert n_meta <= n_meta_pad
    k_meta = jnp.pad(k_meta, ((0, n_meta_pad - n_meta), (0, 0)))
    v_meta = jnp.pad(v_meta, ((0, n_meta_pad - n_meta), (0, 0)))
    vec = lambda a: a.reshape(1, -1).astype(F32)
    small = lambda n: pl.BlockSpec((1, n), lambda b, h, i: (0, 0))
    kern = functools.partial(_attn_kernel, tq=tq, tk=tk, n_meta=n_meta, lambda_init=lambda_init)
    return pl.pallas_call(
        kern,
        grid=(batch, hgroups, nq),
        in_specs=[small(DIFF_QK_DIM)] * 4 + [
            small(dv),
            pl.BlockSpec((tq, bw), lambda b, h, i: (b * nq + i, h)),
            pl.BlockSpec((seq, bw), lambda b, h, i: (b, hgroups + h)),
            pl.BlockSpec((seq, bw), lambda b, h, i: (b, h)),
            pl.BlockSpec((n_meta_pad, bw), lambda b, h, i: (0, h)),
            pl.BlockSpec((n_meta_pad, bw), lambda b, h, i: (0, h)),
        ],
        out_specs=pl.BlockSpec((tq, bw), lambda b, h, i: (b * nq + i, h)),
        out_shape=jax.ShapeDtypeStruct((batch * seq, heads * dv), BF16),
        scratch_shapes=[pltpu.VMEM((2 * hps, tq, dv), F32),
                        pltpu.VMEM((2 * hps, tq, LANES), F32),
                        pltpu.VMEM((2 * hps, tq, LANES), F32)],
        compiler_params=_params(3),
        name="diff_attention",
    )(vec(lq1), vec(lk1), vec(lq2), vec(lk2), vec(subln_g), qk, qk, v, k_meta, v_meta)


def _gelu_tanh(x):
    c = math.sqrt(2.0 / math.pi)
    return 0.5 * x * (1.0 + jnp.tanh(c * (x + 0.044715 * (x * x * x))))


def _sigmoid(x):
    return 0.5 * jnp.tanh(0.5 * x) + 0.5


def _lru_kernel(xr_ref, gate_ref, tail0_ref, h0_ref, cw_ref, cb_ref, wr_ref, br_ref, wi_ref,
                bi_ref, lam_ref, o_ref, hlast_ref, xs_sc, h_sc, *, tt, hc):
    t = pl.program_id(2)
    blk = LANES
    halo = SUBLANES

    @pl.when(t == 0)
    def _():
        xs_sc[0:halo, :] = tail0_ref[...]
        h_sc[...] = h0_ref[...]

    xs_sc[halo:halo + tt, :] = xr_ref[...]
    xc = cb_ref[...]
    for w in range(CONV_WIDTH):
        start = halo - (CONV_WIDTH - 1) + w
        xc = xc + cw_ref[w:w + 1, :] * xs_sc[pl.ds(start, tt), :]
    xs_sc[0:halo, :] = xr_ref[tt - halo:tt, :]

    r_parts, i_parts = [], []
    for hh in range(hc):
        xh = xc[:, hh * blk:(hh + 1) * blk].astype(BF16)
        r_parts.append(jnp.dot(xh, wr_ref[hh].astype(BF16), preferred_element_type=F32))
        i_parts.append(jnp.dot(xh, wi_ref[hh].astype(BF16), preferred_element_type=F32))
    r = _sigmoid(jnp.concatenate(r_parts, axis=1) + br_ref[...])
    ig = _sigmoid(jnp.concatenate(i_parts, axis=1) + bi_ref[...])

    neg_lam = -lam_ref[...]
    softplus = jnp.maximum(neg_lam, 0.0) + jnp.log1p(jnp.exp(-jnp.abs(neg_lam)))
    log_a = (-LRU_C) * r * softplus
    a = jnp.exp(log_a)
    mult = jnp.sqrt(-jnp.tanh(log_a) * (a * a + 1.0))
    b = mult * (ig * xc)

    ngroups = tt // SUBLANES
    a = a.reshape(ngroups, SUBLANES, a.shape[1])
    b = b.reshape(a.shape)
    row = lax.broadcasted_iota(jnp.int32, (1,) + a.shape[1:], 1)
    d = 1
    while d < SUBLANES:
        keep = row >= d
        a_sh = jnp.where(keep, pltpu.roll(a, d, 1), 1.0)
        b_sh = jnp.where(keep, pltpu.roll(b, d, 1), 0.0)
        b = a * b_sh + b
        a = a * a_sh
        d *= 2
    carry = h_sc[...]
    groups = []
    for j in range(ngroups):
        hj = a[j] * carry + b[j]
        groups.append(hj)
        carry = jnp.broadcast_to(hj[SUBLANES - 1:SUBLANES], hj.shape)
    h = jnp.concatenate(groups, axis=0)
    h_sc[...] = carry
    hlast_ref[...] = carry
    o_ref[...] = (h * _gelu_tanh(gate_ref[...])).astype(o_ref.dtype)


def _rg_lru(xg, tail0, h0, conv_w, conv_b, w_r, b_r, w_i, b_i, lru_lambda, *, batch, seq, tt, hc):
    width = conv_w.shape[1]
    cw = hc * LANES
    ngrp = width // cw
    nt = seq // tt
    row = lambda a: a.reshape(1, width).astype(F32)
    chan = lambda r: pl.BlockSpec((r, cw), lambda b, g, t: (0, g))
    wspec = pl.BlockSpec((hc, LANES, LANES), lambda b, g, t: (g, 0, 0))
    kern = functools.partial(_lru_kernel, tt=tt, hc=hc)
    return pl.pallas_call(
        kern,
        grid=(batch, ngrp, nt),
        in_specs=[
            pl.BlockSpec((tt, cw), lambda b, g, t: (b * nt + t, g)),
            pl.BlockSpec((tt, cw), lambda b, g, t: (b * nt + t, ngrp + g)),
            chan(SUBLANES), chan(SUBLANES),
            chan(CONV_WIDTH), chan(1),
            wspec, chan(1), wspec, chan(1), chan(1),
        ],
        out_specs=[pl.BlockSpec((tt, cw), lambda b, g, t: (b * nt + t, g)),
                   pl.BlockSpec((SUBLANES, cw), lambda b, g, t: (b, g))],
        out_shape=[jax.ShapeDtypeStruct((batch * seq, width), BF16),
                   jax.ShapeDtypeStruct((batch * SUBLANES, width), F32)],
        scratch_shapes=[pltpu.VMEM((tt + SUBLANES, cw), F32),
                        pltpu.VMEM((SUBLANES, cw), F32)],
        compiler_params=_params(3),
        name="rg_lru",
    )(xg, xg, tail0, h0, conv_w.astype(F32), row(conv_b), w_r, row(b_r), w_i, row(b_i),
      row(lru_lambda))


def _accumulate_k_tile(a, w_ref, acc, rows, first=False):
    a = a[:, :rows]
    tc = acc.shape[2]
    for e in range(acc.shape[0]):
        w = w_ref[0:rows, e * tc:(e + 1) * tc].astype(BF16)
        d = jnp.dot(a, w, preferred_element_type=F32)
        acc[e] = d if first else acc[e] + d


def _row_rms_scale(acc, n):
    ssq = jnp.sum(acc[0] * acc[0], axis=1, keepdims=True)
    for e in range(1, acc.shape[0]):
        ssq = ssq + jnp.sum(acc[e] * acc[e], axis=1, keepdims=True)
    return jnp.broadcast_to(lax.rsqrt(ssq / n + RMS_EPS), (acc.shape[1], LANES))


def _lane_tile(x, width):
    return jnp.concatenate([x] * (width // x.shape[1]), axis=1)


def _out_proj_kernel(attn_ref, rec_ref, w_ref, x_ref, g1_ref, g2_ref, h1_ref, u2_ref, rs2_ref, acc,
                     rs_sc, ssq_sc, *, nk, ne):
    s = pl.program_id(1)
    tk = attn_ref.shape[1]
    tc = acc.shape[2]
    n = acc.shape[0] * tc

    @pl.when(s == 0)
    def _():
        _accumulate_k_tile(attn_ref[...], w_ref, acc, tk, first=True)

    @pl.when((s > 0) & (s < nk // 2))
    def _():
        _accumulate_k_tile(attn_ref[...], w_ref, acc, tk)

    @pl.when((s >= nk // 2) & (s < nk))
    def _():
        _accumulate_k_tile(rec_ref[...], w_ref, acc, tk)

    @pl.when(s == nk)
    def _():
        rs_sc[...] = _row_rms_scale(acc, n)
        ssq_sc[...] = jnp.zeros_like(ssq_sc)

    @pl.when(s >= nk)
    def _():
        h1 = x_ref[...] + acc[s - nk] * _lane_tile(rs_sc[...], tc) * g1_ref[...]
        h1_ref[...] = h1
        u2_ref[...] = (h1 * g2_ref[...]).astype(u2_ref.dtype)
        ssq_sc[...] += jnp.broadcast_to(jnp.sum(h1 * h1, axis=1, keepdims=True), ssq_sc.shape)

    @pl.when(s == nk + ne - 1)
    def _():
        rs2_ref[...] = lax.rsqrt(ssq_sc[...] / n + RMS_EPS)


def _out_proj(attn, rec, w_out, x2d, g_post, g_pre, *, tm, tk, tc):
    m, k_half = attn.shape
    n = w_out.shape[1]
    assert rec.shape == attn.shape and w_out.shape[0] == 2 * k_half
    assert m % tm == 0 and k_half % tk == 0 and n % tc == 0
    nkh = k_half // tk
    nk = 2 * nkh
    ne = n // tc
    chunk = lambda s: jnp.clip(s - nk, 0, ne - 1)
    kern = functools.partial(_out_proj_kernel, nk=nk, ne=ne)
    return pl.pallas_call(
        kern,
        grid=(m // tm, nk + ne),
        in_specs=[pl.BlockSpec((tm, tk), lambda i, s: (i, jnp.minimum(s, nkh - 1))),
                  pl.BlockSpec((tm, tk), lambda i, s: (i, jnp.clip(s - nkh, 0, nkh - 1))),
                  pl.BlockSpec((tk, n), lambda i, s: (jnp.minimum(s, nk - 1), 0)),
                  pl.BlockSpec((tm, tc), lambda i, s: (i, chunk(s))),
                  pl.BlockSpec((1, tc), lambda i, s: (0, chunk(s))),
                  pl.BlockSpec((1, tc), lambda i, s: (0, chunk(s)))],
        out_specs=[pl.BlockSpec((tm, tc), lambda i, s: (i, chunk(s))),
                   pl.BlockSpec((tm, tc), lambda i, s: (i, chunk(s))),
                   pl.BlockSpec((tm, LANES), lambda i, s: (i, 0))],
        out_shape=[jax.ShapeDtypeStruct((m, n), F32), jax.ShapeDtypeStruct((m, n), BF16),
                   jax.ShapeDtypeStruct((m, LANES), F32)],
        scratch_shapes=[pltpu.VMEM((ne, tm, tc), F32), pltpu.VMEM((tm, LANES), F32),
                        pltpu.VMEM((tm, LANES), F32)],
        compiler_params=_params(2),
        name="out_proj",
    )(attn, rec, w_out, x2d, g_post.reshape(1, n), g_pre.reshape(1, n))


def _ffn_up_kernel(a_ref, rs_ref, wg_ref, wu_ref, o_ref, *, nj, last_cols):
    rs = _lane_tile(rs_ref[...], MXU_DIM)

    def emit(ncols):
        for c in range(ncols // MXU_DIM):
            cols = slice(c * MXU_DIM, (c + 1) * MXU_DIM)
            g = jnp.dot(a_ref[...], wg_ref[:, cols].astype(BF16), preferred_element_type=F32) * rs
            u = jnp.dot(a_ref[...], wu_ref[:, cols].astype(BF16), preferred_element_type=F32) * rs
            o_ref[:, cols] = (g * jax.nn.sigmoid(g) * u).astype(o_ref.dtype)

    tf = o_ref.shape[1]
    if last_cols == tf:
        emit(tf)
    else:
        j = pl.program_id(1)
        pl.when(j < nj - 1)(lambda: emit(tf))
        pl.when(j == nj - 1)(lambda: emit(last_cols))


def _ffn_up(u2, row_scale, w_gate, w_up, *, tm, tf):
    m, k = u2.shape
    f = w_gate.shape[1]
    nj = pl.cdiv(f, tf)
    last_cols = f - (nj - 1) * tf
    assert m % tm == 0 and tf % MXU_DIM == 0 and last_cols % MXU_DIM == 0
    wspec = pl.BlockSpec((k, tf), lambda i, j: (0, j))
    return pl.pallas_call(
        functools.partial(_ffn_up_kernel, nj=nj, last_cols=last_cols),
        grid=(m // tm, nj),
        in_specs=[pl.BlockSpec((tm, k), lambda i, j: (i, 0)),
                  pl.BlockSpec((tm, LANES), lambda i, j: (i, 0)), wspec, wspec],
        out_specs=pl.BlockSpec((tm, tf), lambda i, j: (i, j)),
        out_shape=jax.ShapeDtypeStruct((m, f), BF16),
        compiler_params=_params(2),
        name="ffn_up",
    )(u2, row_scale, w_gate, w_up)


def _ffn_down_kernel(a_ref, w_ref, h1_ref, g_ref, o_ref, acc, rs_sc, *, nk, k_last):
    s = pl.program_id(1)
    tk = a_ref.shape[1]
    n = acc.shape[0] * acc.shape[2]

    @pl.when(s == 0)
    def _():
        _accumulate_k_tile(a_ref[...], w_ref, acc, tk, first=True)

    @pl.when((s > 0) & (s < nk - 1))
    def _():
        _accumulate_k_tile(a_ref[...], w_ref, acc, tk)

    @pl.when(s == nk - 1)
    def _():
        _accumulate_k_tile(a_ref[...], w_ref, acc, k_last)

    @pl.when(s == nk)
    def _():
        rs_sc[...] = _row_rms_scale(acc, n)

    @pl.when(s >= nk)
    def _():
        y = acc[s - nk] * _lane_tile(rs_sc[...], acc.shape[2]) * g_ref[...]
        o_ref[...] = h1_ref[...] + y


def _ffn_down(act, w_down, h1, g, *, tm, tk, tc):
    m, f = act.shape
    n = w_down.shape[1]
    nk = pl.cdiv(f, tk)
    k_last = f - (nk - 1) * tk
    ne = n // tc
    assert m % tm == 0 and n % tc == 0 and k_last % LANES == 0
    kstep = lambda s: jnp.minimum(s, nk - 1)
    chunk = lambda s: jnp.clip(s - nk, 0, ne - 1)
    kern = functools.partial(_ffn_down_kernel, nk=nk, k_last=k_last)
    return pl.pallas_call(
        kern,
        grid=(m // tm, nk + ne),
        in_specs=[pl.BlockSpec((tm, tk), lambda i, s: (i, kstep(s))),
                  pl.BlockSpec((tk, n), lambda i, s: (kstep(s), 0)),
                  pl.BlockSpec((tm, tc), lambda i, s: (i, chunk(s))),
                  pl.BlockSpec((1, tc), lambda i, s: (0, chunk(s)))],
        out_specs=pl.BlockSpec((tm, tc), lambda i, s: (i, chunk(s))),
        out_shape=jax.ShapeDtypeStruct((m, n), F32),
        scratch_shapes=[pltpu.VMEM((ne, tm, tc), F32), pltpu.VMEM((tm, LANES), F32)],
        compiler_params=_params(2),
        name="ffn_down",
    )(act, w_down, h1, g.reshape(1, n))


def _rope_tables(n_pos):
    inv_freq = 1.0 / (ROPE_THETA ** (jnp.arange(0, DIFF_QK_DIM, 2, dtype=F32) / DIFF_QK_DIM))
    ang = jnp.arange(n_pos, dtype=F32)[:, None] * inv_freq[None, :]
    cos, sin = jnp.cos(ang), jnp.sin(ang)
    return jnp.concatenate([cos, cos], axis=-1), jnp.concatenate([-sin, sin], axis=-1)


class _Tiling(NamedTuple):
    rows: int
    cols_bf16: int
    cols_f32: int
    ff_cols: int
    k_resident: int
    chunk_out_proj: int
    chunk_ffn_down: int
    norm_rows: int
    attn_q: int
    attn_kv: int
    attn_heads: int
    lru_rows: int
    lru_heads: int


def _tiling(seq, d_ff):
    t = _Tiling(rows=4 * MXU_DIM, cols_bf16=4 * MXU_DIM, cols_f32=2 * MXU_DIM,
                ff_cols=2 * MXU_DIM, k_resident=2 * MXU_DIM, chunk_out_proj=2 * MXU_DIM,
                chunk_ffn_down=4 * MXU_DIM,
                norm_rows=MXU_DIM, attn_q=4 * MXU_DIM, attn_kv=2 * MXU_DIM, attn_heads=2,
                lru_rows=4 * MXU_DIM, lru_heads=4)
    assert seq % t.rows == 0 and seq % t.attn_q == 0 and seq % t.lru_rows == 0
    assert d_ff % MXU_DIM == 0
    return t


def kernel(x, meta_tokens, mix_pre_g, w_in, lambda_q1, lambda_k1, lambda_q2, lambda_k2, subln_g,
           conv_w, conv_b, w_r, b_r, w_i, b_i, lru_lambda, w_out, mix_post_g, ffn_pre_g, w_gate,
           w_up, w_down, ffn_post_g):
    batch, seq, d = x.shape
    n_meta = meta_tokens.shape[0]
    depth = w_in.shape[0]
    attn_w = d // 2
    lru_w = d - attn_w
    heads = attn_w // DIFF_V_DIM
    rows = batch * seq
    assert depth == 1, "the meta-prefix factorisation below is written for a single layer"
    assert n_meta == 2 * SUBLANES
    t = _tiling(seq, w_gate.shape[2])

    l = 0
    lambda_init = 0.8 - 0.6 * math.exp(-0.3 * l)
    scale = DIFF_QK_DIM ** -0.5 * math.log2(math.e)
    cos_all, sin_all = _rope_tables(n_meta + seq)
    cos_x, sin_x = cos_all[n_meta:], sin_all[n_meta:]
    cos_m, sin_m = cos_all[:n_meta], sin_all[:n_meta]

    x2d = x.reshape(rows, d)
    w_in_l = w_in[l]

    seq_tiles = seq // t.rows
    tab_x = pl.BlockSpec((t.rows, DIFF_QK_DIM), lambda i, j: (i % seq_tiles, 0))
    tab_m = pl.BlockSpec((n_meta, DIFF_QK_DIM), lambda i, j: (0, 0))
    rope = dict(store=_rope_store)
    rope_x = dict(rope, extras=(cos_x, sin_x), extra_specs=(tab_x, tab_x), prepare=functools.partial(
        _rope_prepare, n_scaled=attn_w // t.cols_bf16, scale=scale))
    rope_m = dict(rope, extras=(cos_m, sin_m), extra_specs=(tab_m, tab_m), prepare=functools.partial(
        _rope_prepare, n_scaled=0, scale=scale))

    u_x = _prenorm(x2d, mix_pre_g[l], t.norm_rows)
    u_m = _prenorm(meta_tokens.astype(x.dtype), mix_pre_g[l], n_meta)
    proj_x = functools.partial(_matmul_os, u_x, w_in_l, tm=t.rows)
    proj_m = functools.partial(_matmul_os, u_m, w_in_l, tm=n_meta)
    bf16_out = dict(tn=t.cols_bf16, out_dtype=BF16)
    f32_out = dict(tn=t.cols_f32, out_dtype=F32)
    qk_x = proj_x(0, 2 * attn_w, **bf16_out, **rope_x, name="in_proj_qk")
    v_x = proj_x(2 * attn_w, attn_w, **bf16_out, name="in_proj_v")
    xg_x = proj_x(3 * attn_w, 2 * lru_w, **f32_out, name="in_proj_lru")
    k_m = proj_m(attn_w, attn_w, **bf16_out, **rope_m, name="in_proj_k_meta")
    v_m = proj_m(2 * attn_w, attn_w, **bf16_out, name="in_proj_v_meta")
    xg_m = proj_m(3 * attn_w, 2 * lru_w, **f32_out, name="in_proj_lru_meta")

    attn = _diff_attention(qk_x, v_x, k_m, v_m, lambda_q1[l], lambda_k1[l], lambda_q2[l],
                           lambda_k2[l], subln_g[l], batch=batch, seq=seq, heads=heads,
                           lambda_init=lambda_init, tq=t.attn_q, tk=t.attn_kv, hps=t.attn_heads)

    lru_args = (conv_w[l], conv_b[l], w_r[l], b_r[l], w_i[l], b_i[l], lru_lambda[l])
    zeros8 = jnp.zeros((SUBLANES, lru_w), F32)
    _, h_meta = _rg_lru(xg_m, zeros8, zeros8, *lru_args, batch=1, seq=n_meta, tt=n_meta,
                        hc=t.lru_heads)
    tail_meta = xg_m[n_meta - SUBLANES:, :lru_w]
    rec, _ = _rg_lru(xg_x, tail_meta, h_meta, *lru_args, batch=batch, seq=seq, tt=t.lru_rows,
                     hc=t.lru_heads)

    h1, u2, rs2 = _out_proj(attn, rec, w_out[l], x2d, mix_post_g[l], ffn_pre_g[l], tm=t.rows,
                            tk=t.k_resident, tc=t.chunk_out_proj)

    act = _ffn_up(u2, rs2, w_gate[l], w_up[l], tm=t.rows, tf=t.ff_cols)
    out = _ffn_down(act, w_down[l], h1, ffn_post_g[l], tm=t.rows, tk=t.k_resident,
                    tc=t.chunk_ffn_down)
    return out.reshape(batch, seq, d)
```

```python
import functools
import math
from typing import NamedTuple

import jax
import jax.numpy as jnp
from jax import lax
from jax.experimental import pallas as pl
from jax.experimental.pallas import tpu as pltpu

DIFF_QK_DIM = 128
DIFF_V_DIM = 2 * DIFF_QK_DIM
LRU_HEADS = 16
CONV_WIDTH = 4
LRU_C = 8.0
ROPE_THETA = 10000.0
RMS_EPS = 1e-6

LANES = 128
SUBLANES = 8
MXU_DIM = 256
VMEM_BYTES = 64 * 1024 * 1024
VMEM_LIMIT_BYTES = VMEM_BYTES * 15 // 16

BF16 = jnp.bfloat16
F32 = jnp.float32


def _params(n_axes):
    return pltpu.CompilerParams(
        dimension_semantics=("arbitrary",) * n_axes,
        vmem_limit_bytes=VMEM_LIMIT_BYTES,
    )


def _rms_scale(x):
    return lax.rsqrt(jnp.mean(x * x, axis=-1, keepdims=True) + RMS_EPS)


def _prenorm_kernel(x_ref, g_ref, o_ref):
    x = x_ref[...]
    o_ref[...] = (x * _rms_scale(x) * g_ref[...]).astype(o_ref.dtype)


def _prenorm(x2d, g, tm):
    rows, d = x2d.shape
    return pl.pallas_call(
        _prenorm_kernel,
        grid=(rows // tm,),
        in_specs=[pl.BlockSpec((tm, d), lambda i: (i, 0)),
                  pl.BlockSpec((1, d), lambda i: (0, 0))],
        out_specs=pl.BlockSpec((tm, d), lambda i: (i, 0)),
        out_shape=jax.ShapeDtypeStruct((rows, d), BF16),
        compiler_params=_params(1),
        name="prenorm",
    )(x2d, g.reshape(1, d))


def _mm_kernel(a_ref, w_ref, *rest, prepare, store, n_extra):
    extra = rest[:n_extra]
    o_ref = rest[n_extra]
    ctx = prepare(*extra)
    for c in range(o_ref.shape[1] // MXU_DIM):
        cols = slice(c * MXU_DIM, (c + 1) * MXU_DIM)
        acc = jnp.dot(a_ref[...], w_ref[:, cols].astype(BF16), preferred_element_type=F32)
        store(acc, o_ref, cols, ctx)


def _no_prepare():
    return None


def _plain_store(acc, o_ref, cols, ctx):
    o_ref[:, cols] = acc.astype(o_ref.dtype)


def _rope_prepare(cos_ref, sin_ref, *, n_scaled, scale):
    mult = jnp.where(pl.program_id(1) < n_scaled, jnp.float32(scale), jnp.float32(1.0))
    return cos_ref[...] * mult, sin_ref[...] * mult


def _rope_store(acc, o_ref, cols, tables):
    cos, sin = tables
    for h in range(acc.shape[1] // DIFF_QK_DIM):
        y = acc[:, h * DIFF_QK_DIM:(h + 1) * DIFF_QK_DIM]
        y = y * cos + pltpu.roll(y, DIFF_QK_DIM // 2, 1) * sin
        c0 = cols.start + h * DIFF_QK_DIM
        o_ref[:, c0:c0 + DIFF_QK_DIM] = y.astype(o_ref.dtype)


def _matmul_os(a, w, col0, ncols, *, tm, tn, out_dtype, prepare=_no_prepare, store=_plain_store,
               extras=(), extra_specs=(), name):
    m, k = a.shape
    assert w.shape[0] == k and m % tm == 0 and ncols % tn == 0 and col0 % tn == 0
    assert tn % MXU_DIM == 0
    jb0 = col0 // tn
    kern = functools.partial(_mm_kernel, prepare=prepare, store=store, n_extra=len(extras))
    return pl.pallas_call(
        kern,
        grid=(m // tm, ncols // tn),
        in_specs=[pl.BlockSpec((tm, k), lambda i, j: (i, 0)),
                  pl.BlockSpec((k, tn), lambda i, j: (0, jb0 + j)),
                  *extra_specs],
        out_specs=pl.BlockSpec((tm, tn), lambda i, j: (i, j)),
        out_shape=jax.ShapeDtypeStruct((m, ncols), out_dtype),
        compiler_params=_params(2),
        name=name,
    )(a, w, *extras)


def _dot_nt(a, b):
    return lax.dot_general(a, b, (((1,), (1,)), ((), ())), preferred_element_type=F32)


def _attn_kernel(lq1_ref, lk1_ref, lq2_ref, lk2_ref, g_ref, q_ref, k_ref, v_ref, km_ref, vm_ref,
                 o_ref, acc_sc, m_sc, l_sc, *, tq, tk, n_meta, lambda_init):
    assert tq % tk == 0
    qi = pl.program_id(2)
    dk, dv = DIFF_QK_DIM, DIFF_V_DIM
    n_meta_pad = km_ref.shape[0]
    streams = [(2 * hh + c, slice(hh * dv + c * dk, hh * dv + (c + 1) * dk),
                slice(hh * dv, (hh + 1) * dv))
               for hh in range(q_ref.shape[1] // dv) for c in range(2)]

    def update(c, rows, blocks, first):
        nrows = blocks[0][0].shape[0]
        groups = [s[:, g * LANES:(g + 1) * LANES] for s, _ in blocks
                  for g in range(s.shape[1] // LANES)]
        mx = groups[0]
        for grp in groups[1:]:
            mx = jnp.maximum(mx, grp)
        m_cur = jnp.max(mx, axis=1, keepdims=True)
        if first:
            m_new = jnp.broadcast_to(m_cur, (nrows, LANES))
        else:
            m_prev = m_sc[c, rows]
            m_new = jnp.maximum(m_prev, m_cur)
            alpha = jnp.exp2(m_prev - m_new)
        ps = [jnp.exp2(grp - m_new) for grp in groups]
        lsum = ps[0]
        for p in ps[1:]:
            lsum = lsum + p
        pv = None
        g0 = 0
        for s, v in blocks:
            ng = s.shape[1] // LANES
            p = jnp.concatenate([p.astype(BF16) for p in ps[g0:g0 + ng]], axis=1)
            g0 += ng
            d = jnp.dot(p, v, preferred_element_type=F32)
            pv = d if pv is None else pv + d
        if first:
            l_sc[c, rows] = lsum
            acc_sc[c, rows] = pv
        else:
            l_sc[c, rows] = alpha * l_sc[c, rows] + lsum
            acc_sc[c, rows] = (jnp.concatenate([alpha] * (DIFF_V_DIM // LANES), axis=1)
                               * acc_sc[c, rows] + pv)
        m_sc[c, rows] = m_new

    nsub = tq // tk
    meta_ok = lax.broadcasted_iota(jnp.int32, (tk, n_meta_pad), 1) < n_meta
    causal = (lax.broadcasted_iota(jnp.int32, (tk, tk), 1)
              <= lax.broadcasted_iota(jnp.int32, (tk, tk), 0))
    for r in range(nsub):
        rows = slice(r * tk, (r + 1) * tk)
        for c, cs, vs in streams:
            q = q_ref[rows, cs]
            blocks = [(jnp.where(meta_ok, _dot_nt(q, km_ref[:, cs]), -jnp.inf), vm_ref[:, vs])]
            for j in range(r + 1):
                ks = pl.multiple_of((qi * nsub + j) * tk, tk)
                s = _dot_nt(q, k_ref[pl.ds(ks, tk), cs])
                if j == r:
                    s = jnp.where(causal, s, -jnp.inf)
                blocks.append((s, v_ref[pl.ds(ks, tk), vs]))
            update(c, rows, blocks, True)

    def tile(ks):
        for c, cs, vs in streams:
            s = _dot_nt(q_ref[:, cs], k_ref[pl.ds(ks, tk), cs])
            update(c, slice(0, tq), [(s, v_ref[pl.ds(ks, tk), vs])], False)

    n_full = qi * nsub

    def body(pair, carry):
        tile(pl.multiple_of(2 * pair * tk, tk))
        tile(pl.multiple_of((2 * pair + 1) * tk, tk))
        return carry

    lax.fori_loop(0, n_full // 2, body, 0)

    if nsub % 2 == 1:
        @pl.when(n_full % 2 == 1)
        def _():
            tile(pl.multiple_of((n_full - 1) * tk, tk))

    lam = (jnp.exp(jnp.sum(lq1_ref[...] * lk1_ref[...], axis=1, keepdims=True))
           - jnp.exp(jnp.sum(lq2_ref[...] * lk2_ref[...], axis=1, keepdims=True))
           + lambda_init)
    for c0, _, vs in streams[::2]:
        l0 = jnp.sum(l_sc[c0], axis=1, keepdims=True)
        l1 = jnp.sum(l_sc[c0 + 1], axis=1, keepdims=True)
        o = acc_sc[c0] / l0 - lam * (acc_sc[c0 + 1] / l1)
        o = o * _rms_scale(o) * g_ref[...] * (1.0 - lambda_init)
        o_ref[:, vs] = o.astype(o_ref.dtype)


def _diff_attention(qk, v, k_meta, v_meta, lq1, lk1, lq2, lk2, subln_g, *, batch, seq, heads,
                    lambda_init, tq, tk, hps):
    assert heads % hps == 0
    dv = DIFF_V_DIM
    bw = hps * dv
    hgroups = heads // hps
    nq = seq // tq
    n_meta = k_meta.shape[0]
    n_meta_pad = LANES
    assert n_meta <= n_meta_pad
    k_meta = jnp.pad(k_meta, ((0, n_meta_pad - n_meta), (0, 0)))
    v_meta = jnp.pad(v_meta, ((0, n_meta_pad - n_meta), (0, 0)))
    vec = lambda a: a.reshape(1, -1).astype(F32)
    small = lambda n: pl.BlockSpec((1, n), lambda b, h, i: (0, 0))
    kern = functools.partial(_attn_kernel, tq=tq, tk=tk, n_meta=n_meta, lambda_init=lambda_init)
    return pl.pallas_call(
        kern,
        grid=(batch, hgroups, nq),
        in_specs=[small(DIFF_QK_DIM)] * 4 + [
            small(dv),
            pl.BlockSpec((tq, bw), lambda b, h, i: (b * nq + i, h)),
            pl.BlockSpec((seq, bw), lambda b, h, i: (b, hgroups + h)),
            pl.BlockSpec((seq, bw), lambda b, h, i: (b, h)),
            pl.BlockSpec((n_meta_pad, bw), lambda b, h, i: (0, h)),
            pl.BlockSpec((n_meta_pad, bw), lambda b, h, i: (0, h)),
        ],
        out_specs=pl.BlockSpec((tq, bw), lambda b, h, i: (b * nq + i, h)),
        out_shape=jax.ShapeDtypeStruct((batch * seq, heads * dv), BF16),
        scratch_shapes=[pltpu.VMEM((2 * hps, tq, dv), F32),
                        pltpu.VMEM((2 * hps, tq, LANES), F32),
                        pltpu.VMEM((2 * hps, tq, LANES), F32)],
        compiler_params=_params(3),
        name="diff_attention",
    )(vec(lq1), vec(lk1), vec(lq2), vec(lk2), vec(subln_g), qk, qk, v, k_meta, v_meta)


def _gelu_tanh(x):
    c = math.sqrt(2.0 / math.pi)
    return 0.5 * x * (1.0 + jnp.tanh(c * (x + 0.044715 * (x * x * x))))


def _sigmoid(x):
    return 0.5 * jnp.tanh(0.5 * x) + 0.5


def _lru_kernel(xr_ref, gate_ref, tail0_ref, h0_ref, cw_ref, cb_ref, wr_ref, br_ref, wi_ref,
                bi_ref, lam_ref, o_ref, hlast_ref, xs_sc, h_sc, *, tt, hc):
    t = pl.program_id(2)
    blk = LANES
    halo = SUBLANES

    @pl.when(t == 0)
    def _():
        xs_sc[0:halo, :] = tail0_ref[...]
        h_sc[...] = h0_ref[...]

    xs_sc[halo:halo + tt, :] = xr_ref[...]
    xc = cb_ref[...]
    for w in range(CONV_WIDTH):
        start = halo - (CONV_WIDTH - 1) + w
        xc = xc + cw_ref[w:w + 1, :] * xs_sc[pl.ds(start, tt), :]
    xs_sc[0:halo, :] = xr_ref[tt - halo:tt, :]

    r_parts, i_parts = [], []
    for hh in range(hc):
        xh = xc[:, hh * blk:(hh + 1) * blk].astype(BF16)
        r_parts.append(jnp.dot(xh, wr_ref[hh].astype(BF16), preferred_element_type=F32))
        i_parts.append(jnp.dot(xh, wi_ref[hh].astype(BF16), preferred_element_type=F32))
    r = _sigmoid(jnp.concatenate(r_parts, axis=1) + br_ref[...])
    ig = _sigmoid(jnp.concatenate(i_parts, axis=1) + bi_ref[...])

    neg_lam = -lam_ref[...]
    softplus = jnp.maximum(neg_lam, 0.0) + jnp.log1p(jnp.exp(-jnp.abs(neg_lam)))
    log_a = (-LRU_C) * r * softplus
    a = jnp.exp(log_a)
    mult = jnp.sqrt(-jnp.tanh(log_a) * (a * a + 1.0))
    b = mult * (ig * xc)

    ngroups = tt // SUBLANES
    a = a.reshape(ngroups, SUBLANES, a.shape[1])
    b = b.reshape(a.shape)
    row = lax.broadcasted_iota(jnp.int32, (1,) + a.shape[1:], 1)
    d = 1
    while d < SUBLANES:
        keep = row >= d
        a_sh = jnp.where(keep, pltpu.roll(a, d, 1), 1.0)
        b_sh = jnp.where(keep, pltpu.roll(b, d, 1), 0.0)
        b = a * b_sh + b
        a = a * a_sh
        d *= 2
    carry = h_sc[...]
    groups = []
    for j in range(ngroups):
        hj = a[j] * carry + b[j]
        groups.append(hj)
        carry = jnp.broadcast_to(hj[SUBLANES - 1:SUBLANES], hj.shape)
    h = jnp.concatenate(groups, axis=0)
    h_sc[...] = carry
    hlast_ref[...] = carry
    o_ref[...] = (h * _gelu_tanh(gate_ref[...])).astype(o_ref.dtype)


def _rg_lru(xg, tail0, h0, conv_w, conv_b, w_r, b_r, w_i, b_i, lru_lambda, *, batch, seq, tt, hc):
    width = conv_w.shape[1]
    cw = hc * LANES
    ngrp = width // cw
    nt = seq // tt
    row = lambda a: a.reshape(1, width).astype(F32)
    chan = lambda r: pl.BlockSpec((r, cw), lambda b, g, t: (0, g))
    wspec = pl.BlockSpec((hc, LANES, LANES), lambda b, g, t: (g, 0, 0))
    kern = functools.partial(_lru_kernel, tt=tt, hc=hc)
    return pl.pallas_call(
        kern,
        grid=(batch, ngrp, nt),
        in_specs=[
            pl.BlockSpec((tt, cw), lambda b, g, t: (b * nt + t, g)),
            pl.BlockSpec((tt, cw), lambda b, g, t: (b * nt + t, ngrp + g)),
            chan(SUBLANES), chan(SUBLANES),
            chan(CONV_WIDTH), chan(1),
            wspec, chan(1), wspec, chan(1), chan(1),
        ],
        out_specs=[pl.BlockSpec((tt, cw), lambda b, g, t: (b * nt + t, g)),
                   pl.BlockSpec((SUBLANES, cw), lambda b, g, t: (b, g))],
        out_shape=[jax.ShapeDtypeStruct((batch * seq, width), BF16),
                   jax.ShapeDtypeStruct((batch * SUBLANES, width), F32)],
        scratch_shapes=[pltpu.VMEM((tt + SUBLANES, cw), F32),
                        pltpu.VMEM((SUBLANES, cw), F32)],
        compiler_params=_params(3),
        name="rg_lru",
    )(xg, xg, tail0, h0, conv_w.astype(F32), row(conv_b), w_r, row(b_r), w_i, row(b_i),
      row(lru_lambda))


def _accumulate_k_tile(a, w_ref, acc, rows, first=False):
    a = a[:, :rows]
    tc = acc.shape[2]
    for e in range(acc.shape[0]):
        w = w_ref[0:rows, e * tc:(e + 1) * tc].astype(BF16)
        d = jnp.dot(a, w, preferred_element_type=F32)
        acc[e] = d if first else acc[e] + d


def _row_rms_scale(acc, n):
    ssq = jnp.sum(acc[0] * acc[0], axis=1, keepdims=True)
    for e in range(1, acc.shape[0]):
        ssq = ssq + jnp.sum(acc[e] * acc[e], axis=1, keepdims=True)
    return jnp.broadcast_to(lax.rsqrt(ssq / n + RMS_EPS), (acc.shape[1], LANES))


def _lane_tile(x, width):
    return jnp.concatenate([x] * (width // x.shape[1]), axis=1)


def _out_proj_kernel(attn_ref, rec_ref, w_ref, x_ref, g1_ref, g2_ref, h1_ref, u2_ref, rs2_ref, acc,
                     rs_sc, ssq_sc, *, nk, ne):
    s = pl.program_id(1)
    tk = attn_ref.shape[1]
    tc = acc.shape[2]
    n = acc.shape[0] * tc

    @pl.when(s == 0)
    def _():
        _accumulate_k_tile(attn_ref[...], w_ref, acc, tk, first=True)

    @pl.when((s > 0) & (s < nk // 2))
    def _():
        _accumulate_k_tile(attn_ref[...], w_ref, acc, tk)

    @pl.when((s >= nk // 2) & (s < nk))
    def _():
        _accumulate_k_tile(rec_ref[...], w_ref, acc, tk)

    @pl.when(s == nk)
    def _():
        rs_sc[...] = _row_rms_scale(acc, n)
        ssq_sc[...] = jnp.zeros_like(ssq_sc)

    @pl.when(s >= nk)
    def _():
        h1 = x_ref[...] + acc[s - nk] * _lane_tile(rs_sc[...], tc) * g1_ref[...]
        h1_ref[...] = h1
        u2_ref[...] = (h1 * g2_ref[...]).astype(u2_ref.dtype)
        ssq_sc[...] += jnp.broadcast_to(jnp.sum(h1 * h1, axis=1, keepdims=True), ssq_sc.shape)

    @pl.when(s == nk + ne - 1)
    def _():
        rs2_ref[...] = lax.rsqrt(ssq_sc[...] / n + RMS_EPS)


def _out_proj(attn, rec, w_out, x2d, g_post, g_pre, *, tm, tk, tc):
    m, k_half = attn.shape
    n = w_out.shape[1]
    assert rec.shape == attn.shape and w_out.shape[0] == 2 * k_half
    assert m % tm == 0 and k_half % tk == 0 and n % tc == 0
    nkh = k_half // tk
    nk = 2 * nkh
    ne = n // tc
    chunk = lambda s: jnp.clip(s - nk, 0, ne - 1)
    kern = functools.partial(_out_proj_kernel, nk=nk, ne=ne)
    return pl.pallas_call(
        kern,
        grid=(m // tm, nk + ne),
        in_specs=[pl.BlockSpec((tm, tk), lambda i, s: (i, jnp.minimum(s, nkh - 1))),
                  pl.BlockSpec((tm, tk), lambda i, s: (i, jnp.clip(s - nkh, 0, nkh - 1))),
                  pl.BlockSpec((tk, n), lambda i, s: (jnp.minimum(s, nk - 1), 0)),
                  pl.BlockSpec((tm, tc), lambda i, s: (i, chunk(s))),
                  pl.BlockSpec((1, tc), lambda i, s: (0, chunk(s))),
                  pl.BlockSpec((1, tc), lambda i, s: (0, chunk(s)))],
        out_specs=[pl.BlockSpec((tm, tc), lambda i, s: (i, chunk(s))),
                   pl.BlockSpec((tm, tc), lambda i, s: (i, chunk(s))),
                   pl.BlockSpec((tm, LANES), lambda i, s: (i, 0))],
        out_shape=[jax.ShapeDtypeStruct((m, n), F32), jax.ShapeDtypeStruct((m, n), BF16),
                   jax.ShapeDtypeStruct((m, LANES), F32)],
        scratch_shapes=[pltpu.VMEM((ne, tm, tc), F32), pltpu.VMEM((tm, LANES), F32),
                        pltpu.VMEM((tm, LANES), F32)],
        compiler_params=_params(2),
        name="out_proj",
    )(attn, rec, w_out, x2d, g_post.reshape(1, n), g_pre.reshape(1, n))


def _ffn_up_kernel(a_ref, rs_ref, wg_ref, wu_ref, o_ref, *, nj, last_cols):
    rs = _lane_tile(rs_ref[...], MXU_DIM)

    def emit(ncols):
        for c in range(ncols // MXU_DIM):
            cols = slice(c * MXU_DIM, (c + 1) * MXU_DIM)
            g = jnp.dot(a_ref[...], wg_ref[:, cols].astype(BF16), preferred_element_type=F32) * rs
            u = jnp.dot(a_ref[...], wu_ref[:, cols].astype(BF16), preferred_element_type=F32) * rs
            o_ref[:, cols] = (g * jax.nn.sigmoid(g) * u).astype(o_ref.dtype)

    tf = o_ref.shape[1]
    if last_cols == tf:
        emit(tf)
    else:
        j = pl.program_id(1)
        pl.when(j < nj - 1)(lambda: emit(tf))
        pl.when(j == nj - 1)(lambda: emit(last_cols))


def _ffn_up(u2, row_scale, w_gate, w_up, *, tm, tf):
    m, k = u2.shape
    f = w_gate.shape[1]
    nj = pl.cdiv(f, tf)
    last_cols = f - (nj - 1) * tf
    assert m % tm == 0 and tf % MXU_DIM == 0 and last_cols % MXU_DIM == 0
    wspec = pl.BlockSpec((k, tf), lambda i, j: (0, j))
    return pl.pallas_call(
        functools.partial(_ffn_up_kernel, nj=nj, last_cols=last_cols),
        grid=(m // tm, nj),
        in_specs=[pl.BlockSpec((tm, k), lambda i, j: (i, 0)),
                  pl.BlockSpec((tm, LANES), lambda i, j: (i, 0)), wspec, wspec],
        out_specs=pl.BlockSpec((tm, tf), lambda i, j: (i, j)),
        out_shape=jax.ShapeDtypeStruct((m, f), BF16),
        compiler_params=_params(2),
        name="ffn_up",
    )(u2, row_scale, w_gate, w_up)


def _ffn_down_kernel(a_ref, w_ref, h1_ref, g_ref, o_ref, acc, rs_sc, *, nk, k_last):
    s = pl.program_id(1)
    tk = a_ref.shape[1]
    n = acc.shape[0] * acc.shape[2]

    @pl.when(s == 0)
    def _():
        _accumulate_k_tile(a_ref[...], w_ref, acc, tk, first=True)

    @pl.when((s > 0) & (s < nk - 1))
    def _():
        _accumulate_k_tile(a_ref[...], w_ref, acc, tk)

    @pl.when(s == nk - 1)
    def _():
        _accumulate_k_tile(a_ref[...], w_ref, acc, k_last)

    @pl.when(s == nk)
    def _():
        rs_sc[...] = _row_rms_scale(acc, n)

    @pl.when(s >= nk)
    def _():
        y = acc[s - nk] * _lane_tile(rs_sc[...], acc.shape[2]) * g_ref[...]
        o_ref[...] = h1_ref[...] + y


def _ffn_down(act, w_down, h1, g, *, tm, tk, tc):
    m, f = act.shape
    n = w_down.shape[1]
    nk = pl.cdiv(f, tk)
    k_last = f - (nk - 1) * tk
    ne = n // tc
    assert m % tm == 0 and n % tc == 0 and k_last % LANES == 0
    kstep = lambda s: jnp.minimum(s, nk - 1)
    chunk = lambda s: jnp.clip(s - nk, 0, ne - 1)
    kern = functools.partial(_ffn_down_kernel, nk=nk, k_last=k_last)
    return pl.pallas_call(
        kern,
        grid=(m // tm, nk + ne),
        in_specs=[pl.BlockSpec((tm, tk), lambda i, s: (i, kstep(s))),
                  pl.BlockSpec((tk, n), lambda i, s: (kstep(s), 0)),
                  pl.BlockSpec((tm, tc), lambda i, s: (i, chunk(s))),
                  pl.BlockSpec((1, tc), lambda i, s: (0, chunk(s)))],
        out_specs=pl.BlockSpec((tm, tc), lambda i, s: (i, chunk(s))),
        out_shape=jax.ShapeDtypeStruct((m, n), F32),
        scratch_shapes=[pltpu.VMEM((ne, tm, tc), F32), pltpu.VMEM((tm, LANES), F32)],
        compiler_params=_params(2),
        name="ffn_down",
    )(act, w_down, h1, g.reshape(1, n))


def _rope_tables(n_pos):
    inv_freq = 1.0 / (ROPE_THETA ** (jnp.arange(0, DIFF_QK_DIM, 2, dtype=F32) / DIFF_QK_DIM))
    ang = jnp.arange(n_pos, dtype=F32)[:, None] * inv_freq[None, :]
    cos, sin = jnp.cos(ang), jnp.sin(ang)
    return jnp.concatenate([cos, cos], axis=-1), jnp.concatenate([-sin, sin], axis=-1)


class _Tiling(NamedTuple):
    rows: int
    cols_bf16: int
    cols_f32: int
    ff_cols: int
    k_resident: int
    chunk_out_proj: int
    chunk_ffn_down: int
    norm_rows: int
    attn_q: int
    attn_kv: int
    attn_heads: int
    lru_rows: int
    lru_heads: int


def _tiling(seq, d_ff):
    t = _Tiling(rows=4 * MXU_DIM, cols_bf16=4 * MXU_DIM, cols_f32=4 * MXU_DIM,
                ff_cols=2 * MXU_DIM, k_resident=2 * MXU_DIM, chunk_out_proj=4 * MXU_DIM,
                chunk_ffn_down=4 * MXU_DIM,
                norm_rows=MXU_DIM, attn_q=4 * MXU_DIM, attn_kv=2 * MXU_DIM, attn_heads=2,
                lru_rows=4 * MXU_DIM, lru_heads=4)
    assert seq % t.rows == 0 and seq % t.attn_q == 0 and seq % t.lru_rows == 0
    assert d_ff % MXU_DIM == 0
    return t


def kernel(x, meta_tokens, mix_pre_g, w_in, lambda_q1, lambda_k1, lambda_q2, lambda_k2, subln_g,
           conv_w, conv_b, w_r, b_r, w_i, b_i, lru_lambda, w_out, mix_post_g, ffn_pre_g, w_gate,
           w_up, w_down, ffn_post_g):
    batch, seq, d = x.shape
    n_meta = meta_tokens.shape[0]
    depth = w_in.shape[0]
    attn_w = d // 2
    lru_w = d - attn_w
    heads = attn_w // DIFF_V_DIM
    rows = batch * seq
    assert depth == 1, "the meta-prefix factorisation below is written for a single layer"
    assert n_meta == 2 * SUBLANES
    t = _tiling(seq, w_gate.shape[2])

    l = 0
    lambda_init = 0.8 - 0.6 * math.exp(-0.3 * l)
    scale = DIFF_QK_DIM ** -0.5 * math.log2(math.e)
    cos_all, sin_all = _rope_tables(n_meta + seq)
    cos_x, sin_x = cos_all[n_meta:], sin_all[n_meta:]
    cos_m, sin_m = cos_all[:n_meta], sin_all[:n_meta]

    x2d = x.reshape(rows, d)
    w_in_l = w_in[l]

    seq_tiles = seq // t.rows
    tab_x = pl.BlockSpec((t.rows, DIFF_QK_DIM), lambda i, j: (i % seq_tiles, 0))
    tab_m = pl.BlockSpec((n_meta, DIFF_QK_DIM), lambda i, j: (0, 0))
    rope = dict(store=_rope_store)
    rope_x = dict(rope, extras=(cos_x, sin_x), extra_specs=(tab_x, tab_x), prepare=functools.partial(
        _rope_prepare, n_scaled=attn_w // t.cols_bf16, scale=scale))
    rope_m = dict(rope, extras=(cos_m, sin_m), extra_specs=(tab_m, tab_m), prepare=functools.partial(
        _rope_prepare, n_scaled=0, scale=scale))

    u_x = _prenorm(x2d, mix_pre_g[l], t.norm_rows)
    u_m = _prenorm(meta_tokens.astype(x.dtype), mix_pre_g[l], n_meta)
    proj_x = functools.partial(_matmul_os, u_x, w_in_l, tm=t.rows)
    proj_m = functools.partial(_matmul_os, u_m, w_in_l, tm=n_meta)
    bf16_out = dict(tn=t.cols_bf16, out_dtype=BF16)
    f32_out = dict(tn=t.cols_f32, out_dtype=F32)
    qk_x = proj_x(0, 2 * attn_w, **bf16_out, **rope_x, name="in_proj_qk")
    v_x = proj_x(2 * attn_w, attn_w, **bf16_out, name="in_proj_v")
    xg_x = proj_x(3 * attn_w, 2 * lru_w, **f32_out, name="in_proj_lru")
    k_m = proj_m(attn_w, attn_w, **bf16_out, **rope_m, name="in_proj_k_meta")
    v_m = proj_m(2 * attn_w, attn_w, **bf16_out, name="in_proj_v_meta")
    xg_m = proj_m(3 * attn_w, 2 * lru_w, **f32_out, name="in_proj_lru_meta")

    attn = _diff_attention(qk_x, v_x, k_m, v_m, lambda_q1[l], lambda_k1[l], lambda_q2[l],
                           lambda_k2[l], subln_g[l], batch=batch, seq=seq, heads=heads,
                           lambda_init=lambda_init, tq=t.attn_q, tk=t.attn_kv, hps=t.attn_heads)

    lru_args = (conv_w[l], conv_b[l], w_r[l], b_r[l], w_i[l], b_i[l], lru_lambda[l])
    zeros8 = jnp.zeros((SUBLANES, lru_w), F32)
    _, h_meta = _rg_lru(xg_m, zeros8, zeros8, *lru_args, batch=1, seq=n_meta, tt=n_meta,
                        hc=t.lru_heads)
    tail_meta = xg_m[n_meta - SUBLANES:, :lru_w]
    rec, _ = _rg_lru(xg_x, tail_meta, h_meta, *lru_args, batch=batch, seq=seq, tt=t.lru_rows,
                     hc=t.lru_heads)

    h1, u2, rs2 = _out_proj(attn, rec, w_out[l], x2d, mix_post_g[l], ffn_pre_g[l], tm=t.rows,
                            tk=t.k_resident, tc=t.chunk_out_proj)

    act = _ffn_up(u2, rs2, w_gate[l], w_up[l], tm=t.rows, tf=t.ff_cols)
    out = _ffn_down(act, w_down[l], h1, ffn_post_g[l], tm=t.rows, tk=t.k_resident,
                    tc=t.chunk_ffn_down)
    return out.reshape(batch, seq, d)
```

```python
import functools
import math
from typing import NamedTuple

import jax
import jax.numpy as jnp
from jax import lax
from jax.experimental import pallas as pl
from jax.experimental.pallas import tpu as pltpu

DIFF_QK_DIM = 128
DIFF_V_DIM = 2 * DIFF_QK_DIM
LRU_HEADS = 16
CONV_WIDTH = 4
LRU_C = 8.0
ROPE_THETA = 10000.0
RMS_EPS = 1e-6

LANES = 128
SUBLANES = 8
MXU_DIM = 256
VMEM_BYTES = 64 * 1024 * 1024
VMEM_LIMIT_BYTES = VMEM_BYTES * 15 // 16

BF16 = jnp.bfloat16
F32 = jnp.float32


def _params(n_axes):
    return pltpu.CompilerParams(
        dimension_semantics=("arbitrary",) * n_axes,
        vmem_limit_bytes=VMEM_LIMIT_BYTES,
    )


def _rms_scale(x):
    return lax.rsqrt(jnp.mean(x * x, axis=-1, keepdims=True) + RMS_EPS)


def _prenorm_kernel(x_ref, g_ref, o_ref):
    x = x_ref[...]
    o_ref[...] = (x * _rms_scale(x) * g_ref[...]).astype(o_ref.dtype)


def _prenorm(x2d, g, tm):
    rows, d = x2d.shape
    return pl.pallas_call(
        _prenorm_kernel,
        grid=(rows // tm,),
        in_specs=[pl.BlockSpec((tm, d), lambda i: (i, 0)),
                  pl.BlockSpec((1, d), lambda i: (0, 0))],
        out_specs=pl.BlockSpec((tm, d), lambda i: (i, 0)),
        out_shape=jax.ShapeDtypeStruct((rows, d), BF16),
        compiler_params=_params(1),
        name="prenorm",
    )(x2d, g.reshape(1, d))


def _mm_kernel(a_ref, w_ref, *rest, prepare, store, n_extra):
    extra = rest[:n_extra]
    o_ref = rest[n_extra]
    ctx = prepare(*extra)
    for c in range(o_ref.shape[1] // MXU_DIM):
        cols = slice(c * MXU_DIM, (c + 1) * MXU_DIM)
        acc = jnp.dot(a_ref[...], w_ref[:, cols].astype(BF16), preferred_element_type=F32)
        store(acc, o_ref, cols, ctx)


def _no_prepare():
    return None


def _plain_store(acc, o_ref, cols, ctx):
    o_ref[:, cols] = acc.astype(o_ref.dtype)


def _rope_prepare(cos_ref, sin_ref, *, n_scaled, scale):
    mult = jnp.where(pl.program_id(1) < n_scaled, jnp.float32(scale), jnp.float32(1.0))
    return cos_ref[...] * mult, sin_ref[...] * mult


def _rope_store(acc, o_ref, cols, tables):
    cos, sin = tables
    for h in range(acc.shape[1] // DIFF_QK_DIM):
        y = acc[:, h * DIFF_QK_DIM:(h + 1) * DIFF_QK_DIM]
        y = y * cos + pltpu.roll(y, DIFF_QK_DIM // 2, 1) * sin
        c0 = cols.start + h * DIFF_QK_DIM
        o_ref[:, c0:c0 + DIFF_QK_DIM] = y.astype(o_ref.dtype)


def _matmul_os(a, w, col0, ncols, *, tm, tn, out_dtype, prepare=_no_prepare, store=_plain_store,
               extras=(), extra_specs=(), name):
    m, k = a.shape
    assert w.shape[0] == k and m % tm == 0 and ncols % tn == 0 and col0 % tn == 0
    assert tn % MXU_DIM == 0
    jb0 = col0 // tn
    kern = functools.partial(_mm_kernel, prepare=prepare, store=store, n_extra=len(extras))
    return pl.pallas_call(
        kern,
        grid=(m // tm, ncols // tn),
        in_specs=[pl.BlockSpec((tm, k), lambda i, j: (i, 0)),
                  pl.BlockSpec((k, tn), lambda i, j: (0, jb0 + j)),
                  *extra_specs],
        out_specs=pl.BlockSpec((tm, tn), lambda i, j: (i, j)),
        out_shape=jax.ShapeDtypeStruct((m, ncols), out_dtype),
        compiler_params=_params(2),
        name=name,
    )(a, w, *extras)


def _dot_nt(a, b):
    return lax.dot_general(a, b, (((1,), (1,)), ((), ())), preferred_element_type=F32)


def _attn_kernel(lq1_ref, lk1_ref, lq2_ref, lk2_ref, g_ref, q_ref, k_ref, v_ref, km_ref, vm_ref,
                 o_ref, acc_sc, m_sc, l_sc, *, tq, tk, n_meta, lambda_init):
    assert tq % tk == 0
    qi = pl.program_id(2)
    dk, dv = DIFF_QK_DIM, DIFF_V_DIM
    n_meta_pad = km_ref.shape[0]
    streams = [(2 * hh + c, slice(hh * dv + c * dk, hh * dv + (c + 1) * dk),
                slice(hh * dv, (hh + 1) * dv))
               for hh in range(q_ref.shape[1] // dv) for c in range(2)]

    def update(c, rows, blocks, first):
        nrows = blocks[0][0].shape[0]
        groups = [s[:, g * LANES:(g + 1) * LANES] for s, _ in blocks
                  for g in range(s.shape[1] // LANES)]
        mx = groups[0]
        for grp in groups[1:]:
            mx = jnp.maximum(mx, grp)
        m_cur = jnp.max(mx, axis=1, keepdims=True)
        if first:
            m_new = jnp.broadcast_to(m_cur, (nrows, LANES))
        else:
            m_prev = m_sc[c, rows]
            m_new = jnp.maximum(m_prev, m_cur)
            alpha = jnp.exp2(m_prev - m_new)
        ps = [jnp.exp2(grp - m_new) for grp in groups]
        lsum = ps[0]
        for p in ps[1:]:
            lsum = lsum + p
        pv = None
        g0 = 0
        for s, v in blocks:
            ng = s.shape[1] // LANES
            p = jnp.concatenate([p.astype(BF16) for p in ps[g0:g0 + ng]], axis=1)
            g0 += ng
            d = jnp.dot(p, v, preferred_element_type=F32)
            pv = d if pv is None else pv + d
        if first:
            l_sc[c, rows] = lsum
            acc_sc[c, rows] = pv
        else:
            l_sc[c, rows] = alpha * l_sc[c, rows] + lsum
            acc_sc[c, rows] = (jnp.concatenate([alpha] * (DIFF_V_DIM // LANES), axis=1)
                               * acc_sc[c, rows] + pv)
        m_sc[c, rows] = m_new

    nsub = tq // tk
    meta_ok = lax.broadcasted_iota(jnp.int32, (tk, n_meta_pad), 1) < n_meta
    causal = (lax.broadcasted_iota(jnp.int32, (tk, tk), 1)
              <= lax.broadcasted_iota(jnp.int32, (tk, tk), 0))
    for r in range(nsub):
        rows = slice(r * tk, (r + 1) * tk)
        for c, cs, vs in streams:
            q = q_ref[rows, cs]
            blocks = [(jnp.where(meta_ok, _dot_nt(q, km_ref[:, cs]), -jnp.inf), vm_ref[:, vs])]
            for j in range(r + 1):
                ks = pl.multiple_of((qi * nsub + j) * tk, tk)
                s = _dot_nt(q, k_ref[pl.ds(ks, tk), cs])
                if j == r:
                    s = jnp.where(causal, s, -jnp.inf)
                blocks.append((s, v_ref[pl.ds(ks, tk), vs]))
            update(c, rows, blocks, True)

    def tile(ks):
        for c, cs, vs in streams:
            s = _dot_nt(q_ref[:, cs], k_ref[pl.ds(ks, tk), cs])
            update(c, slice(0, tq), [(s, v_ref[pl.ds(ks, tk), vs])], False)

    n_full = qi * nsub

    def body(pair, carry):
        tile(pl.multiple_of(2 * pair * tk, tk))
        tile(pl.multiple_of((2 * pair + 1) * tk, tk))
        return carry

    lax.fori_loop(0, n_full // 2, body, 0)

    if nsub % 2 == 1:
        @pl.when(n_full % 2 == 1)
        def _():
            tile(pl.multiple_of((n_full - 1) * tk, tk))

    lam = (jnp.exp(jnp.sum(lq1_ref[...] * lk1_ref[...], axis=1, keepdims=True))
           - jnp.exp(jnp.sum(lq2_ref[...] * lk2_ref[...], axis=1, keepdims=True))
           + lambda_init)
    for c0, _, vs in streams[::2]:
        l0 = jnp.sum(l_sc[c0], axis=1, keepdims=True)
        l1 = jnp.sum(l_sc[c0 + 1], axis=1, keepdims=True)
        o = acc_sc[c0] / l0 - lam * (acc_sc[c0 + 1] / l1)
        o = o * _rms_scale(o) * g_ref[...] * (1.0 - lambda_init)
        o_ref[:, vs] = o.astype(o_ref.dtype)


def _diff_attention(qk, v, k_meta, v_meta, lq1, lk1, lq2, lk2, subln_g, *, batch, seq, heads,
                    lambda_init, tq, tk, hps):
    assert heads % hps == 0
    dv = DIFF_V_DIM
    bw = hps * dv
    hgroups = heads // hps
    nq = seq // tq
    n_meta = k_meta.shape[0]
    n_meta_pad = LANES
    assert n_meta <= n_meta_pad
    k_meta = jnp.pad(k_meta, ((0, n_meta_pad - n_meta), (0, 0)))
    v_meta = jnp.pad(v_meta, ((0, n_meta_pad - n_meta), (0, 0)))
    vec = lambda a: a.reshape(1, -1).astype(F32)
    small = lambda n: pl.BlockSpec((1, n), lambda b, h, i: (0, 0))
    kern = functools.partial(_attn_kernel, tq=tq, tk=tk, n_meta=n_meta, lambda_init=lambda_init)
    return pl.pallas_call(
        kern,
        grid=(batch, hgroups, nq),
        in_specs=[small(DIFF_QK_DIM)] * 4 + [
            small(dv),
            pl.BlockSpec((tq, bw), lambda b, h, i: (b * nq + i, h)),
            pl.BlockSpec((seq, bw), lambda b, h, i: (b, hgroups + h)),
            pl.BlockSpec((seq, bw), lambda b, h, i: (b, h)),
            pl.BlockSpec((n_meta_pad, bw), lambda b, h, i: (0, h)),
            pl.BlockSpec((n_meta_pad, bw), lambda b, h, i: (0, h)),
        ],
        out_specs=pl.BlockSpec((tq, bw), lambda b, h, i: (b * nq + i, h)),
        out_shape=jax.ShapeDtypeStruct((batch * seq, heads * dv), BF16),
        scratch_shapes=[pltpu.VMEM((2 * hps, tq, dv), F32),
                        pltpu.VMEM((2 * hps, tq, LANES), F32),
                        pltpu.VMEM((2 * hps, tq, LANES), F32)],
        compiler_params=_params(3),
        name="diff_attention",
    )(vec(lq1), vec(lk1), vec(lq2), vec(lk2), vec(subln_g), qk, qk, v, k_meta, v_meta)


def _gelu_tanh(x):
    c = math.sqrt(2.0 / math.pi)
    return 0.5 * x * (1.0 + jnp.tanh(c * (x + 0.044715 * (x * x * x))))


def _sigmoid(x):
    return 0.5 * jnp.tanh(0.5 * x) + 0.5


def _lru_kernel(xr_ref, tail0_ref, h0_ref, cw_ref, cb_ref, wr_ref, br_ref, wi_ref, bi_ref, lam_ref,
                *rest, tt, hc, emit):
    if emit:
        gate_ref, hlast_ref, o_ref, xs_sc, h_sc = rest
    else:
        hlast_ref, xs_sc, h_sc = rest
    t = pl.program_id(2)
    blk = LANES
    halo = SUBLANES

    @pl.when(t == 0)
    def _():
        xs_sc[0:halo, :] = tail0_ref[...]
        h_sc[...] = h0_ref[...]

    xs_sc[halo:halo + tt, :] = xr_ref[...]
    xc = cb_ref[...]
    for w in range(CONV_WIDTH):
        start = halo - (CONV_WIDTH - 1) + w
        xc = xc + cw_ref[w:w + 1, :] * xs_sc[pl.ds(start, tt), :]
    xs_sc[0:halo, :] = xr_ref[tt - halo:tt, :]

    r_parts, i_parts = [], []
    for hh in range(hc):
        xh = xc[:, hh * blk:(hh + 1) * blk].astype(BF16)
        r_parts.append(jnp.dot(xh, wr_ref[hh].astype(BF16), preferred_element_type=F32))
        i_parts.append(jnp.dot(xh, wi_ref[hh].astype(BF16), preferred_element_type=F32))
    r = _sigmoid(jnp.concatenate(r_parts, axis=1) + br_ref[...])
    ig = _sigmoid(jnp.concatenate(i_parts, axis=1) + bi_ref[...])

    neg_lam = -lam_ref[...]
    softplus = jnp.maximum(neg_lam, 0.0) + jnp.log1p(jnp.exp(-jnp.abs(neg_lam)))
    log_a = (-LRU_C) * r * softplus
    a = jnp.exp(log_a)
    mult = jnp.sqrt(-jnp.tanh(log_a) * (a * a + 1.0))
    b = mult * (ig * xc)

    ngroups = tt // SUBLANES
    a = a.reshape(ngroups, SUBLANES, a.shape[1])
    b = b.reshape(a.shape)
    row = lax.broadcasted_iota(jnp.int32, (1,) + a.shape[1:], 1)
    d = 1
    while d < SUBLANES:
        keep = row >= d
        a_sh = jnp.where(keep, pltpu.roll(a, d, 1), 1.0)
        b_sh = jnp.where(keep, pltpu.roll(b, d, 1), 0.0)
        b = a * b_sh + b
        a = a * a_sh
        d *= 2
    carry = h_sc[...]
    groups = []
    for j in range(ngroups):
        hj = a[j] * carry + b[j]
        groups.append(hj)
        carry = jnp.broadcast_to(hj[SUBLANES - 1:SUBLANES], hj.shape)
    h = jnp.concatenate(groups, axis=0)
    h_sc[...] = carry
    hlast_ref[...] = carry
    if emit:
        o_ref[...] = (h * _gelu_tanh(gate_ref[...])).astype(o_ref.dtype)


def _rg_lru(xg, tail0, h0, conv_w, conv_b, w_r, b_r, w_i, b_i, lru_lambda, *, batch, seq, tt, hc,
            emit=True):
    width = conv_w.shape[1]
    assert xg.shape[1] == (2 if emit else 1) * width
    cw = hc * LANES
    ngrp = width // cw
    nt = seq // tt
    row = lambda a: a.reshape(1, width).astype(F32)
    chan = lambda r: pl.BlockSpec((r, cw), lambda b, g, t: (0, g))
    wspec = pl.BlockSpec((hc, LANES, LANES), lambda b, g, t: (g, 0, 0))
    kern = functools.partial(_lru_kernel, tt=tt, hc=hc, emit=emit)
    seq_spec = lambda col0: pl.BlockSpec((tt, cw), lambda b, g, t: (b * nt + t, col0 + g))
    outs = pl.pallas_call(
        kern,
        grid=(batch, ngrp, nt),
        in_specs=[
            seq_spec(0),
            chan(SUBLANES), chan(SUBLANES),
            chan(CONV_WIDTH), chan(1),
            wspec, chan(1), wspec, chan(1), chan(1),
        ] + ([seq_spec(ngrp)] if emit else []),
        out_specs=[pl.BlockSpec((SUBLANES, cw), lambda b, g, t: (b, g))]
        + ([seq_spec(0)] if emit else []),
        out_shape=[jax.ShapeDtypeStruct((batch * SUBLANES, width), F32)]
        + ([jax.ShapeDtypeStruct((batch * seq, width), BF16)] if emit else []),
        scratch_shapes=[pltpu.VMEM((tt + SUBLANES, cw), F32),
                        pltpu.VMEM((SUBLANES, cw), F32)],
        compiler_params=_params(3),
        name="rg_lru",
    )(xg, tail0, h0, conv_w.astype(F32), row(conv_b), w_r, row(b_r), w_i, row(b_i),
      row(lru_lambda), *([xg] if emit else []))
    return (outs[1], outs[0]) if emit else (None, outs[0])


def _accumulate_k_tile(a, w_ref, acc, rows, first=False):
    a = a[:, :rows]
    tc = acc.shape[2]
    for e in range(acc.shape[0]):
        w = w_ref[0:rows, e * tc:(e + 1) * tc].astype(BF16)
        d = jnp.dot(a, w, preferred_element_type=F32)
        acc[e] = d if first else acc[e] + d


def _row_rms_scale(acc, n):
    ssq = jnp.sum(acc[0] * acc[0], axis=1, keepdims=True)
    for e in range(1, acc.shape[0]):
        ssq = ssq + jnp.sum(acc[e] * acc[e], axis=1, keepdims=True)
    return jnp.broadcast_to(lax.rsqrt(ssq / n + RMS_EPS), (acc.shape[1], LANES))


def _lane_tile(x, width):
    return jnp.concatenate([x] * (width // x.shape[1]), axis=1)


def _out_proj_kernel(attn_ref, rec_ref, w_ref, x_ref, g1_ref, g2_ref, h1_ref, u2_ref, rs2_ref, acc,
                     rs_sc, ssq_sc, *, nk, ne):
    s = pl.program_id(1)
    tk = attn_ref.shape[1]
    tc = acc.shape[2]
    n = acc.shape[0] * tc

    @pl.when(s == 0)
    def _():
        _accumulate_k_tile(attn_ref[...], w_ref, acc, tk, first=True)

    @pl.when((s > 0) & (s < nk // 2))
    def _():
        _accumulate_k_tile(attn_ref[...], w_ref, acc, tk)

    @pl.when((s >= nk // 2) & (s < nk))
    def _():
        _accumulate_k_tile(rec_ref[...], w_ref, acc, tk)

    @pl.when(s == nk)
    def _():
        rs_sc[...] = _row_rms_scale(acc, n)
        ssq_sc[...] = jnp.zeros_like(ssq_sc)

    @pl.when(s >= nk)
    def _():
        h1 = x_ref[...] + acc[s - nk] * _lane_tile(rs_sc[...], tc) * g1_ref[...]
        h1_ref[...] = h1
        u2_ref[...] = (h1 * g2_ref[...]).astype(u2_ref.dtype)
        ssq_sc[...] += jnp.broadcast_to(jnp.sum(h1 * h1, axis=1, keepdims=True), ssq_sc.shape)

    @pl.when(s == nk + ne - 1)
    def _():
        rs2_ref[...] = lax.rsqrt(ssq_sc[...] / n + RMS_EPS)


def _out_proj(attn, rec, w_out, x2d, g_post, g_pre, *, tm, tk, tc):
    m, k_half = attn.shape
    n = w_out.shape[1]
    assert rec.shape == attn.shape and w_out.shape[0] == 2 * k_half
    assert m % tm == 0 and k_half % tk == 0 and n % tc == 0
    nkh = k_half // tk
    nk = 2 * nkh
    ne = n // tc
    chunk = lambda s: jnp.clip(s - nk, 0, ne - 1)
    kern = functools.partial(_out_proj_kernel, nk=nk, ne=ne)
    return pl.pallas_call(
        kern,
        grid=(m // tm, nk + ne),
        in_specs=[pl.BlockSpec((tm, tk), lambda i, s: (i, jnp.minimum(s, nkh - 1))),
                  pl.BlockSpec((tm, tk), lambda i, s: (i, jnp.clip(s - nkh, 0, nkh - 1))),
                  pl.BlockSpec((tk, n), lambda i, s: (jnp.minimum(s, nk - 1), 0)),
                  pl.BlockSpec((tm, tc), lambda i, s: (i, chunk(s))),
                  pl.BlockSpec((1, tc), lambda i, s: (0, chunk(s))),
                  pl.BlockSpec((1, tc), lambda i, s: (0, chunk(s)))],
        out_specs=[pl.BlockSpec((tm, tc), lambda i, s: (i, chunk(s))),
                   pl.BlockSpec((tm, tc), lambda i, s: (i, chunk(s))),
                   pl.BlockSpec((tm, LANES), lambda i, s: (i, 0))],
        out_shape=[jax.ShapeDtypeStruct((m, n), F32), jax.ShapeDtypeStruct((m, n), BF16),
                   jax.ShapeDtypeStruct((m, LANES), F32)],
        scratch_shapes=[pltpu.VMEM((ne, tm, tc), F32), pltpu.VMEM((tm, LANES), F32),
                        pltpu.VMEM((tm, LANES), F32)],
        compiler_params=_params(2),
        name="out_proj",
    )(attn, rec, w_out, x2d, g_post.reshape(1, n), g_pre.reshape(1, n))


def _ffn_up_kernel(a_ref, rs_ref, wg_ref, wu_ref, o_ref, *, nj, last_cols):
    rs = _lane_tile(rs_ref[...], MXU_DIM)

    def emit(ncols):
        for c in range(ncols // MXU_DIM):
            cols = slice(c * MXU_DIM, (c + 1) * MXU_DIM)
            g = jnp.dot(a_ref[...], wg_ref[:, cols].astype(BF16), preferred_element_type=F32) * rs
            u = jnp.dot(a_ref[...], wu_ref[:, cols].astype(BF16), preferred_element_type=F32) * rs
            o_ref[:, cols] = (g * jax.nn.sigmoid(g) * u).astype(o_ref.dtype)

    tf = o_ref.shape[1]
    if last_cols == tf:
        emit(tf)
    else:
        j = pl.program_id(1)
        pl.when(j < nj - 1)(lambda: emit(tf))
        pl.when(j == nj - 1)(lambda: emit(last_cols))


def _ffn_up(u2, row_scale, w_gate, w_up, *, tm, tf):
    m, k = u2.shape
    f = w_gate.shape[1]
    nj = pl.cdiv(f, tf)
    last_cols = f - (nj - 1) * tf
    assert m % tm == 0 and tf % MXU_DIM == 0 and last_cols % MXU_DIM == 0
    wspec = pl.BlockSpec((k, tf), lambda i, j: (0, j))
    return pl.pallas_call(
        functools.partial(_ffn_up_kernel, nj=nj, last_cols=last_cols),
        grid=(m // tm, nj),
        in_specs=[pl.BlockSpec((tm, k), lambda i, j: (i, 0)),
                  pl.BlockSpec((tm, LANES), lambda i, j: (i, 0)), wspec, wspec],
        out_specs=pl.BlockSpec((tm, tf), lambda i, j: (i, j)),
        out_shape=jax.ShapeDtypeStruct((m, f), BF16),
        compiler_params=_params(2),
        name="ffn_up",
    )(u2, row_scale, w_gate, w_up)


def _ffn_down_kernel(a_ref, w_ref, h1_ref, g_ref, o_ref, acc, rs_sc, *, nk, k_last):
    s = pl.program_id(1)
    tk = a_ref.shape[1]
    n = acc.shape[0] * acc.shape[2]

    @pl.when(s == 0)
    def _():
        _accumulate_k_tile(a_ref[...], w_ref, acc, tk, first=True)

    @pl.when((s > 0) & (s < nk - 1))
    def _():
        _accumulate_k_tile(a_ref[...], w_ref, acc, tk)

    @pl.when(s == nk - 1)
    def _():
        _accumulate_k_tile(a_ref[...], w_ref, acc, k_last)

    @pl.when(s == nk)
    def _():
        rs_sc[...] = _row_rms_scale(acc, n)

    @pl.when(s >= nk)
    def _():
        y = acc[s - nk] * _lane_tile(rs_sc[...], acc.shape[2]) * g_ref[...]
        o_ref[...] = h1_ref[...] + y


def _ffn_down(act, w_down, h1, g, *, tm, tk, tc):
    m, f = act.shape
    n = w_down.shape[1]
    nk = pl.cdiv(f, tk)
    k_last = f - (nk - 1) * tk
    ne = n // tc
    assert m % tm == 0 and n % tc == 0 and k_last % LANES == 0
    kstep = lambda s: jnp.minimum(s, nk - 1)
    chunk = lambda s: jnp.clip(s - nk, 0, ne - 1)
    kern = functools.partial(_ffn_down_kernel, nk=nk, k_last=k_last)
    return pl.pallas_call(
        kern,
        grid=(m // tm, nk + ne),
        in_specs=[pl.BlockSpec((tm, tk), lambda i, s: (i, kstep(s))),
                  pl.BlockSpec((tk, n), lambda i, s: (kstep(s), 0)),
                  pl.BlockSpec((tm, tc), lambda i, s: (i, chunk(s))),
                  pl.BlockSpec((1, tc), lambda i, s: (0, chunk(s)))],
        out_specs=pl.BlockSpec((tm, tc), lambda i, s: (i, chunk(s))),
        out_shape=jax.ShapeDtypeStruct((m, n), F32),
        scratch_shapes=[pltpu.VMEM((ne, tm, tc), F32), pltpu.VMEM((tm, LANES), F32)],
        compiler_params=_params(2),
        name="ffn_down",
    )(act, w_down, h1, g.reshape(1, n))


def _rope_tables(n_pos):
    inv_freq = 1.0 / (ROPE_THETA ** (jnp.arange(0, DIFF_QK_DIM, 2, dtype=F32) / DIFF_QK_DIM))
    ang = jnp.arange(n_pos, dtype=F32)[:, None] * inv_freq[None, :]
    cos, sin = jnp.cos(ang), jnp.sin(ang)
    return jnp.concatenate([cos, cos], axis=-1), jnp.concatenate([-sin, sin], axis=-1)


class _Tiling(NamedTuple):
    rows: int
    cols_bf16: int
    cols_f32: int
    ff_cols: int
    k_resident: int
    chunk_out_proj: int
    chunk_ffn_down: int
    norm_rows: int
    attn_q: int
    attn_kv: int
    attn_heads: int
    lru_rows: int
    lru_heads: int


def _tiling(seq, d_ff):
    t = _Tiling(rows=4 * MXU_DIM, cols_bf16=4 * MXU_DIM, cols_f32=4 * MXU_DIM,
                ff_cols=2 * MXU_DIM, k_resident=2 * MXU_DIM, chunk_out_proj=4 * MXU_DIM,
                chunk_ffn_down=4 * MXU_DIM,
                norm_rows=2 * MXU_DIM, attn_q=4 * MXU_DIM, attn_kv=2 * MXU_DIM, attn_heads=2,
                lru_rows=4 * MXU_DIM, lru_heads=4)
    assert seq % t.rows == 0 and seq % t.attn_q == 0 and seq % t.lru_rows == 0
    assert d_ff % MXU_DIM == 0
    return t


def kernel(x, meta_tokens, mix_pre_g, w_in, lambda_q1, lambda_k1, lambda_q2, lambda_k2, subln_g,
           conv_w, conv_b, w_r, b_r, w_i, b_i, lru_lambda, w_out, mix_post_g, ffn_pre_g, w_gate,
           w_up, w_down, ffn_post_g):
    batch, seq, d = x.shape
    n_meta = meta_tokens.shape[0]
    depth = w_in.shape[0]
    attn_w = d // 2
    lru_w = d - attn_w
    heads = attn_w // DIFF_V_DIM
    rows = batch * seq
    assert depth == 1, "the meta-prefix factorisation below is written for a single layer"
    assert n_meta == 2 * SUBLANES
    t = _tiling(seq, w_gate.shape[2])

    l = 0
    lambda_init = 0.8 - 0.6 * math.exp(-0.3 * l)
    scale = DIFF_QK_DIM ** -0.5 * math.log2(math.e)
    cos_all, sin_all = _rope_tables(n_meta + seq)
    cos_x, sin_x = cos_all[n_meta:], sin_all[n_meta:]
    cos_m, sin_m = cos_all[:n_meta], sin_all[:n_meta]

    x2d = x.reshape(rows, d)
    w_in_l = w_in[l]

    seq_tiles = seq // t.rows
    tab_x = pl.BlockSpec((t.rows, DIFF_QK_DIM), lambda i, j: (i % seq_tiles, 0))
    tab_m = pl.BlockSpec((n_meta, DIFF_QK_DIM), lambda i, j: (0, 0))
    rope = dict(store=_rope_store)
    rope_x = dict(rope, extras=(cos_x, sin_x), extra_specs=(tab_x, tab_x), prepare=functools.partial(
        _rope_prepare, n_scaled=attn_w // t.cols_bf16, scale=scale))
    rope_m = dict(rope, extras=(cos_m, sin_m), extra_specs=(tab_m, tab_m), prepare=functools.partial(
        _rope_prepare, n_scaled=0, scale=scale))

    u_x = _prenorm(x2d, mix_pre_g[l], t.norm_rows)
    u_m = _prenorm(meta_tokens.astype(x.dtype), mix_pre_g[l], n_meta)
    proj_x = functools.partial(_matmul_os, u_x, w_in_l, tm=t.rows)
    proj_m = functools.partial(_matmul_os, u_m, w_in_l, tm=n_meta)
    bf16_out = dict(tn=t.cols_bf16, out_dtype=BF16)
    f32_out = dict(tn=t.cols_f32, out_dtype=F32)
    qk_x = proj_x(0, 2 * attn_w, **bf16_out, **rope_x, name="in_proj_qk")
    v_x = proj_x(2 * attn_w, attn_w, **bf16_out, name="in_proj_v")
    xg_x = proj_x(3 * attn_w, 2 * lru_w, **f32_out, name="in_proj_lru")
    k_m = proj_m(attn_w, attn_w, **bf16_out, **rope_m, name="in_proj_k_meta")
    v_m = proj_m(2 * attn_w, attn_w, **bf16_out, name="in_proj_v_meta")
    xr_m = proj_m(3 * attn_w, lru_w, **f32_out, name="in_proj_lru_meta")

    attn = _diff_attention(qk_x, v_x, k_m, v_m, lambda_q1[l], lambda_k1[l], lambda_q2[l],
                           lambda_k2[l], subln_g[l], batch=batch, seq=seq, heads=heads,
                           lambda_init=lambda_init, tq=t.attn_q, tk=t.attn_kv, hps=t.attn_heads)

    lru_args = (conv_w[l], conv_b[l], w_r[l], b_r[l], w_i[l], b_i[l], lru_lambda[l])
    zeros8 = jnp.zeros((SUBLANES, lru_w), F32)
    _, h_meta = _rg_lru(xr_m, zeros8, zeros8, *lru_args, batch=1, seq=n_meta, tt=n_meta,
                        hc=t.lru_heads, emit=False)
    tail_meta = xr_m[n_meta - SUBLANES:]
    rec, _ = _rg_lru(xg_x, tail_meta, h_meta, *lru_args, batch=batch, seq=seq, tt=t.lru_rows,
                     hc=t.lru_heads)

    h1, u2, rs2 = _out_proj(attn, rec, w_out[l], x2d, mix_post_g[l], ffn_pre_g[l], tm=t.rows,
                            tk=t.k_resident, tc=t.chunk_out_proj)

    act = _ffn_up(u2, rs2, w_gate[l], w_up[l], tm=t.rows, tf=t.ff_cols)
    out = _ffn_down(act, w_down[l], h1, ffn_post_g[l], tm=t.rows, tk=t.k_resident,
                    tc=t.chunk_ffn_down)
    return out.reshape(batch, seq, d)
```
